```python
import math
import jax, jax.numpy as jnp
from jax import lax
import numpy as np

D_MODEL = 1024
BATCH = 16
SEQ = 4096
DEPTH = 2

D_SSD = D_MODEL
SSD_HEADDIM = 64
SSD_HEADS = D_SSD // SSD_HEADDIM
SSD_GROUPS = 4
HEADS_PER_GROUP = SSD_HEADS // SSD_GROUPS
SSD_STATE = 128
CHUNK = 128
CONV_WIDTH = 3
CONV_XBC = D_SSD + 2 * SSD_GROUPS * SSD_STATE
D_SC = D_MODEL
SC_GROUPS = 16
D_MIX = D_SSD + D_SC
D_FF = ((8 * D_MODEL // 3 + 255) // 256) * 256
RMS_EPS = 1e-5
DT_MIN = 1e-3
DT_MAX = 1e-1
IN_COLS = D_SSD + CONV_XBC + 2 * SSD_HEADS + 3 * D_SC
SPLIT_POINTS = (D_SSD,
                D_SSD + CONV_XBC,
                D_SSD + CONV_XBC + SSD_HEADS,
                D_SSD + CONV_XBC + 2 * SSD_HEADS,
                D_SSD + CONV_XBC + 2 * SSD_HEADS + D_SC,
                D_SSD + CONV_XBC + 2 * SSD_HEADS + 2 * D_SC)

kernel_name = 'bidir_hybrid_ssd_shortconv_convffn'


def rms_norm(x, w):
    xf = x.astype(jnp.float32)
    y = xf * lax.rsqrt(jnp.mean(xf * xf, axis=-1, keepdims=True) + RMS_EPS)
    return (y * w.astype(jnp.float32)).astype(x.dtype)


def dwconv_centred(u, w, b=None):
    k = w.shape[0]
    half = k // 2
    length = u.shape[1]
    up = jnp.pad(u, ((0, 0), (half, half), (0, 0)))
    out = up[:, 0:length] * w[0]
    for i in range(1, k):
        out = out + up[:, i:i + length] * w[i]
    if b is not None:
        out = out + b
    return out


def ssd_chunked(xdt, a, bm, cm):
    bsz, length, g, r, p = xdt.shape
    n = bm.shape[-1]
    c = length // CHUNK
    xdt = xdt.reshape(bsz, c, CHUNK, g, r, p)
    a = a.reshape(bsz, c, CHUNK, g, r)
    bm = bm.reshape(bsz, c, CHUNK, g, n)
    cm = cm.reshape(bsz, c, CHUNK, g, n)
    a_cs = jnp.cumsum(a, axis=2)
    seg = a_cs[:, :, :, None] - a_cs[:, :, None, :]
    mask = jnp.tril(jnp.ones((CHUNK, CHUNK), dtype=bool))[:, :, None, None]
    decay_ls = jnp.exp(jnp.where(mask, seg, -jnp.inf))
    scores = jnp.einsum('bclgn,bcsgn->bclsg', cm, bm)
    y_diag = jnp.einsum('bclsgr,bcsgrp->bclgrp', scores[..., None] * decay_ls, xdt)
    decay_to_end = jnp.exp(a_cs[:, :, -1:] - a_cs)
    states = jnp.einsum('bclgn,bclgrp->bcgrpn', bm, xdt * decay_to_end[..., None])
    chunk_decay = jnp.exp(a_cs[:, :, -1])

    def step(hstate, inp):
        s_c, d_c = inp
        return hstate * d_c[..., None, None] + s_c, hstate

    h0 = jnp.zeros((bsz, g, r, p, n), dtype=xdt.dtype)
    _, h_prev = lax.scan(step, h0, (jnp.moveaxis(states, 1, 0), jnp.moveaxis(chunk_decay, 1, 0)))
    h_prev = jnp.moveaxis(h_prev, 0, 1)
    y_off = jnp.einsum('bclgn,bcgrpn->bclgrp', cm, h_prev) * jnp.exp(a_cs)[..., None]
    return (y_diag + y_off).reshape(bsz, length, g, r, p)


def ssd_branch(z, xbc, dt_f_raw, dt_b_raw, conv_w, conv_b, dt_bias_f, dt_bias_b,
               a_log_f, a_log_b, d_skip, norm_w):
    f32 = jnp.float32
    bsz, length, _ = z.shape
    xbc = jax.nn.silu(dwconv_centred(xbc, conv_w, conv_b))
    xs, bm, cm = jnp.split(xbc, [D_SSD, D_SSD + SSD_GROUPS * SSD_STATE], axis=-1)
    xs = xs.astype(f32).reshape(bsz, length, SSD_GROUPS, HEADS_PER_GROUP, SSD_HEADDIM)
    bm = bm.astype(f32).reshape(bsz, length, SSD_GROUPS, SSD_STATE)
    cm = cm.astype(f32).reshape(bsz, length, SSD_GROUPS, SSD_STATE)

    def discretise(dt_raw, dt_bias, a_log):
        dt = jax.nn.softplus(dt_raw.astype(f32) + dt_bias.astype(f32))
        dt = dt.reshape(bsz, length, SSD_GROUPS, HEADS_PER_GROUP)
        a = -jnp.exp(a_log.astype(f32)).reshape(SSD_GROUPS, HEADS_PER_GROUP) * dt
        return xs * dt[..., None], a

    xdt_f, a_f = discretise(dt_f_raw, dt_bias_f, a_log_f)
    xdt_b, a_b = discretise(dt_b_raw, dt_bias_b, a_log_b)
    flip = lambda t: jnp.flip(t, axis=1)
    y_f = ssd_chunked(xdt_f, a_f, bm, cm)
    y_b = flip(ssd_chunked(flip(xdt_b), flip(a_b), flip(bm), flip(cm)))
    y = y_f + y_b + xs * d_skip.astype(f32).reshape(SSD_GROUPS, HEADS_PER_GROUP)[..., None]
    y = y.reshape(bsz, length, D_SSD) * jax.nn.silu(z.astype(f32))
    yg = y.reshape(bsz, length, SSD_GROUPS, D_SSD // SSD_GROUPS)
    yg = yg * lax.rsqrt(jnp.mean(yg * yg, axis=-1, keepdims=True) + RMS_EPS)
    return (yg.reshape(bsz, length, D_SSD) * norm_w.astype(f32)).astype(z.dtype)


def short_conv_branch(sc_b, sc_c, sc_v, conv_w):
    return sc_b * dwconv_centred(sc_c * sc_v, conv_w)


def conv_ffn(n, w_up, conv_w, conv_b, w_down):
    u = jnp.einsum('bld,df->blf', n, w_up)
    u = dwconv_centred(u, conv_w, conv_b)
    gate, up = jnp.split(u, 2, axis=-1)
    return jnp.einsum('blf,fd->bld', jax.nn.silu(gate) * up, w_down)


def _dt_bias_init(k, shape):
    u = jax.random.uniform(k, shape, jnp.float32)
    dt = jnp.exp(u * (math.log(DT_MAX) - math.log(DT_MIN)) + math.log(DT_MIN))
    return dt + jnp.log(-jnp.expm1(-dt))


def setup_inputs(seed: int = 0) -> dict:
    key = jax.random.key(seed)
    ks = jax.random.split(key, 20)
    f32 = jnp.float32
    nrm = lambda k, shape, scale: jax.random.normal(k, shape, f32) * scale
    gain = lambda k, shape: 1.0 + 0.01 * jax.random.normal(k, shape, f32)
    return {
        'x': jax.random.normal(ks[0], (BATCH, SEQ, D_MODEL), f32),
        'w_in': nrm(ks[1], (DEPTH, D_MODEL, IN_COLS), D_MODEL ** -0.5),
        'conv_xbc_w': nrm(ks[2], (DEPTH, CONV_WIDTH, CONV_XBC), CONV_WIDTH ** -0.5),
        'conv_xbc_b': nrm(ks[3], (DEPTH, CONV_XBC), 0.01),
        'dt_bias_f': _dt_bias_init(ks[4], (DEPTH, SSD_HEADS)),
        'dt_bias_b': _dt_bias_init(ks[5], (DEPTH, SSD_HEADS)),
        'a_log_f': jnp.log(jax.random.uniform(ks[6], (DEPTH, SSD_HEADS), f32, 1.0, 16.0)),
        'a_log_b': jnp.log(jax.random.uniform(ks[7], (DEPTH, SSD_HEADS), f32, 1.0, 16.0)),
        'd_skip': gain(ks[8], (DEPTH, SSD_HEADS)),
        'ssd_norm_w': gain(ks[9], (DEPTH, D_SSD)),
        'sc_conv_w': nrm(ks[10], (DEPTH, CONV_WIDTH, D_SC), CONV_WIDTH ** -0.5),
        'w_out': nrm(ks[11], (DEPTH, D_MIX, D_MODEL), D_MIX ** -0.5),
        'norm1_w': gain(ks[12], (DEPTH, D_MODEL)),
        'norm2_w': gain(ks[13], (DEPTH, D_MODEL)),
        'w_ffn_up': nrm(ks[14], (DEPTH, D_MODEL, 2 * D_FF), D_MODEL ** -0.5),
        'ffn_conv_w': nrm(ks[15], (DEPTH, CONV_WIDTH, 2 * D_FF), CONV_WIDTH ** -0.5),
        'ffn_conv_b': nrm(ks[16], (DEPTH, 2 * D_FF), 0.01),
        'w_ffn_down': nrm(ks[17], (DEPTH, D_FF, D_MODEL), D_FF ** -0.5),
        'final_norm_w': gain(ks[18], (D_MODEL,)),
    }


def reference(x, w_in, conv_xbc_w, conv_xbc_b, dt_bias_f, dt_bias_b, a_log_f, a_log_b,
              d_skip, ssd_norm_w, sc_conv_w, w_out, norm1_w, norm2_w, w_ffn_up,
              ffn_conv_w, ffn_conv_b, w_ffn_down, final_norm_w):
    h = x
    for l in range(DEPTH):
        n = rms_norm(h, norm1_w[l])
        proj = jnp.einsum('bld,de->ble', n, w_in[l])
        z, xbc, dt_f, dt_b, sc_b, sc_c, sc_v = jnp.split(proj, SPLIT_POINTS, axis=-1)
        y_ssd = ssd_branch(z, xbc, dt_f, dt_b, conv_xbc_w[l], conv_xbc_b[l],
                           dt_bias_f[l], dt_bias_b[l], a_log_f[l], a_log_b[l],
                           d_skip[l], ssd_norm_w[l])
        y_sc = short_conv_branch(sc_b, sc_c, sc_v, sc_conv_w[l])
        mix = jnp.concatenate([y_ssd, y_sc], axis=-1)
        h = h + jnp.einsum('ble,ed->bld', mix, w_out[l])
        n = rms_norm(h, norm2_w[l])
        h = h + conv_ffn(n, w_ffn_up[l], ffn_conv_w[l], ffn_conv_b[l], w_ffn_down[l])
    return rms_norm(h, final_norm_w)
```

```python
import functools

import jax
import jax.numpy as jnp
from jax import lax
from jax.experimental import pallas as pl
from jax.experimental.pallas import tpu as pltpu

F32 = jnp.float32
BF16 = jnp.bfloat16

RMS_EPS = 1e-5
SSD_HEADDIM = 64
SSD_GROUPS = 4
SSD_STATE = 128
CHUNK = 128
HEADS_PER_GROUP = 4
SSD_HEADS = SSD_GROUPS * HEADS_PER_GROUP
GROUP_WIDTH = HEADS_PER_GROUP * SSD_HEADDIM

LANES = 128
F32_SUBLANES = 8
BF16_SUBLANES = 16
HALO = BF16_SUBLANES
VMEM_LIMIT_BYTES = 56 * 1024 * 1024
MASKED_EXPONENT = -1e30


def _dot(a, b):
    return jnp.dot(a, b, preferred_element_type=F32)


def _dot_nt(a, b):
    return lax.dot_general(a, b, (((1,), (1,)), ((), ())), preferred_element_type=F32)


def _dot_tn(a, b):
    return lax.dot_general(a, b, (((0,), (0,)), ((), ())), preferred_element_type=F32)


def _silu(x):
    return x * (1.0 / (1.0 + jnp.exp(-x)))


def _rms(v, w):
    ms = jnp.mean(v * v, axis=-1, keepdims=True)
    return v * lax.rsqrt(ms + RMS_EPS) * w


def _const_spec(shape):
    zeros = (0,) * len(shape)
    return pl.BlockSpec(shape, lambda *_: zeros, pipeline_mode=pl.Buffered(1))


def _halo_lhs(prev_ref, main_ref, next_ref, norm_w, lhs_ref):
    t = pl.program_id(1)
    last = pl.num_programs(1) - 1
    d = main_ref.shape[-1]
    prev = _rms(prev_ref[0], norm_w) * (t > 0).astype(F32)
    nxt = _rms(next_ref[0], norm_w) * (t < last).astype(F32)
    main = _rms(main_ref[0], norm_w)
    zeros = jnp.zeros((HALO - F32_SUBLANES, d), F32)
    lhs_ref[...] = jnp.concatenate([zeros, prev, main, nxt, zeros], axis=0).astype(BF16)


def _conv3(p, w_ref, rows):
    return (p[HALO - 1:HALO - 1 + rows] * w_ref[0:1, :]
            + p[HALO:HALO + rows] * w_ref[1:2, :]
            + p[HALO + 1:HALO + 1 + rows] * w_ref[2:3, :])


IN_COL_BLOCK = 512


def _inproj_kernel(hp_ref, h_ref, hn_ref, nw_ref, wz_ref, wxbc_ref, wscb_ref, wscc_ref,
                   wscv_ref, wdt_ref, cw_ref, cb_ref, scw_ref,
                   z_out, xbc_out, ysc_out, dt_out, lhs_ref):
    rows = h_ref.shape[1]
    _halo_lhs(hp_ref, h_ref, hn_ref, nw_ref[...], lhs_ref)
    centre = lhs_ref[HALO:HALO + rows, :]
    full = lhs_ref[...]

    dt_out[0] = _dot(centre, wdt_ref[...])
    for c in range(0, wz_ref.shape[1], IN_COL_BLOCK):
        cs = slice(c, c + IN_COL_BLOCK)
        z_out[0, :, cs] = _dot(centre, wz_ref[:, cs]).astype(z_out.dtype)
    for c in range(0, wxbc_ref.shape[1], IN_COL_BLOCK):
        cs = slice(c, c + IN_COL_BLOCK)
        p = _dot(full, wxbc_ref[:, cs])
        conv = _conv3(p, cw_ref.at[:, cs], rows) + cb_ref[:, cs]
        xbc_out[0, :, cs] = _silu(conv).astype(xbc_out.dtype)
    for c in range(0, wscb_ref.shape[1], IN_COL_BLOCK):
        cs = slice(c, c + IN_COL_BLOCK)
        cv = _dot(full, wscc_ref[:, cs]) * _dot(full, wscv_ref[:, cs])
        gate = _dot(centre, wscb_ref[:, cs])
        ysc_out[0, :, cs] = (gate * _conv3(cv, scw_ref.at[:, cs], rows)).astype(ysc_out.dtype)


def _inproj(h, norm_w, wz, wxbc, wscb, wscc, wscv, wdt, conv_w, conv_b, sc_w, *, tile):
    bsz, length, d = h.shape
    n_tiles = length // tile
    blk8 = tile // F32_SUBLANES
    last8 = length // F32_SUBLANES - 1
    grid = (bsz, n_tiles)
    row_spec = lambda width: pl.BlockSpec((1, tile, width), lambda b, t: (b, t, 0))
    in_specs = [
        pl.BlockSpec((1, F32_SUBLANES, d), lambda b, t: (b, jnp.maximum(t * blk8 - 1, 0), 0)),
        row_spec(d),
        pl.BlockSpec((1, F32_SUBLANES, d), lambda b, t: (b, jnp.minimum((t + 1) * blk8, last8), 0)),
        _const_spec(norm_w.shape), _const_spec(wz.shape), _const_spec(wxbc.shape),
        _const_spec(wscb.shape), _const_spec(wscc.shape), _const_spec(wscv.shape),
        _const_spec(wdt.shape), _const_spec(conv_w.shape), _const_spec(conv_b.shape),
        _const_spec(sc_w.shape),
    ]
    out_shape = (
        jax.ShapeDtypeStruct((bsz, length, wz.shape[1]), BF16),
        jax.ShapeDtypeStruct((bsz, length, wxbc.shape[1]), BF16),
        jax.ShapeDtypeStruct((bsz, length, wscb.shape[1]), BF16),
        jax.ShapeDtypeStruct((bsz, length, LANES), F32),
    )
    out_specs = (row_spec(wz.shape[1]), row_spec(wxbc.shape[1]), row_spec(wscb.shape[1]),
                 row_spec(LANES))
    return pl.pallas_call(
        _inproj_kernel, grid=grid, in_specs=in_specs, out_specs=out_specs, out_shape=out_shape,
        scratch_shapes=[pltpu.VMEM((tile + 2 * HALO, d), BF16)],
        compiler_params=pltpu.CompilerParams(
            dimension_semantics=("parallel", "parallel"), vmem_limit_bytes=VMEM_LIMIT_BYTES),
        name="inproj",
    )(h, h, h, norm_w, wz, wxbc, wscb, wscc, wscv, wdt, conv_w, conv_b, sc_w)


def _split_hi_lo(v):
    hi = v.astype(BF16)
    lo = (v - hi.astype(F32)).astype(BF16)
    return hi, lo


def _cumsum_rows(tri, a):
    hi = a.astype(BF16)
    r1 = a - hi.astype(F32)
    mid = r1.astype(BF16)
    lo = (r1 - mid.astype(F32)).astype(BF16)
    return _dot(tri, hi) + _dot(tri, mid) + _dot(tri, lo)


def _expand_heads(v, e):
    hi, lo = _split_hi_lo(v)
    return _dot(hi, e) + _dot(lo, e)


def _discretise(raw, bias, a_log):
    x = raw + bias
    dt = jnp.maximum(x, 0.0) + jnp.log(1.0 + jnp.exp(-jnp.abs(x)))
    col = lax.broadcasted_iota(jnp.int32, a_log.shape, 1)
    neg_a = jnp.where(col < 2 * SSD_HEADS, -jnp.exp(a_log), 0.0)
    return dt, dt * neg_a


def _ssd_kernel(dskip_ref, xf_ref, xb_ref, dtf_ref, dtb_ref, bias_ref, alog_ref, tri_ref,
                ef_ref, eb_ref, yf_ref, yb_ref, hf_ref, hb_ref):
    d_ssd = SSD_HEADS * SSD_HEADDIM
    b_off = d_ssd
    c_off = d_ssd + SSD_GROUPS * SSD_STATE

    @pl.when(pl.program_id(1) == 0)
    def _():
        hf_ref[...] = jnp.zeros_like(hf_ref)
        hb_ref[...] = jnp.zeros_like(hb_ref)

    tri = tri_ref[...]
    bias = bias_ref[...]
    a_log = alog_ref[...]
    row_i = lax.broadcasted_iota(jnp.int32, (CHUNK, CHUNK), 0)
    col_i = lax.broadcasted_iota(jnp.int32, (CHUNK, CHUNK), 1)
    lower = row_i >= col_i
    upper = col_i >= row_i
    diag = row_i == col_i
    lane = lax.broadcasted_iota(jnp.int32, (CHUNK, LANES), 1)
    is_fwd_col = lane < SSD_HEADS

    dt, a = _discretise(dtf_ref[0], bias, a_log)
    acs = _cumsum_rows(tri, a)
    tot = acs[CHUNK - 1:CHUNK, :]
    cs = jnp.where(is_fwd_col, acs, acs - a)
    cs_t = cs.T
    dt_t = dt.T

    e_f = ef_ref[...]
    grow_f = _expand_heads(jnp.exp(acs), e_f)
    xscale_f = _expand_heads(dt * jnp.exp(tot - acs), e_f)
    hdecay_f = _expand_heads(jnp.broadcast_to(jnp.exp(tot), (F32_SUBLANES, LANES)), e_f)[0:1]

    head_lane = lax.broadcasted_iota(jnp.int32, (CHUNK, GROUP_WIDTH), 1) // SSD_HEADDIM
    for g in range(SSD_GROUPS):
        gs = slice(g * GROUP_WIDTH, (g + 1) * GROUP_WIDTH)
        xs = xf_ref[0, :, gs]
        bm = xf_ref[0, :, b_off + g * SSD_STATE:b_off + (g + 1) * SSD_STATE]
        cm = xf_ref[0, :, c_off + g * SSD_STATE:c_off + (g + 1) * SSD_STATE]
        scores = _dot_nt(cm, bm)
        w_heads = []
        x_blocks = []
        for r in range(HEADS_PER_GROUP):
            hd = g * HEADS_PER_GROUP + r
            hb = SSD_HEADS + hd
            seg_f = cs[:, hd:hd + 1] - cs_t[hd:hd + 1, :]
            dec_f = jnp.exp(jnp.where(lower, seg_f, MASKED_EXPONENT)) * dt_t[hd:hd + 1, :]
            seg_b = cs_t[hb:hb + 1, :] - cs[:, hb:hb + 1]
            dec_b = jnp.exp(jnp.where(upper, seg_b, MASKED_EXPONENT)) * dt_t[hb:hb + 1, :]
            w = scores * (dec_f + dec_b) + jnp.where(diag, dskip_ref[hd], 0.0)
            w_heads.append(w.astype(BF16))
            x_blocks.append(jnp.where(head_lane == r, xs, jnp.zeros_like(xs)))
        y = _dot(jnp.concatenate(w_heads, axis=1), jnp.concatenate(x_blocks, axis=0))
        y = y + _dot(cm, hf_ref[g].astype(BF16)) * grow_f[:, gs]
        yf_ref[0, :, gs] = y.astype(yf_ref.dtype)
        xt = (xs.astype(F32) * xscale_f[:, gs]).astype(BF16)
        hf_ref[g] = hf_ref[g] * hdecay_f[:, gs] + _dot_tn(bm, xt)

    dt, a = _discretise(dtb_ref[0], bias, a_log)
    acs = _cumsum_rows(tri, a)
    tot = acs[CHUNK - 1:CHUNK, :]
    ecs = acs - a
    e_b = eb_ref[...]
    grow_b = _expand_heads(jnp.exp(tot - ecs), e_b)
    xscale_b = _expand_heads(dt * jnp.exp(ecs), e_b)
    hdecay_b = _expand_heads(jnp.broadcast_to(jnp.exp(tot), (F32_SUBLANES, LANES)), e_b)[0:1]
    for g in range(SSD_GROUPS):
        gs = slice(g * GROUP_WIDTH, (g + 1) * GROUP_WIDTH)
        xs = xb_ref[0, :, gs]
        bm = xb_ref[0, :, b_off + g * SSD_STATE:b_off + (g + 1) * SSD_STATE]
        cm = xb_ref[0, :, c_off + g * SSD_STATE:c_off + (g + 1) * SSD_STATE]
        y = _dot(cm, hb_ref[g].astype(BF16)) * grow_b[:, gs]
        yb_ref[0, :, gs] = y.astype(yb_ref.dtype)
        xt = (xs.astype(F32) * xscale_b[:, gs]).astype(BF16)
        hb_ref[g] = hb_ref[g] * hdecay_b[:, gs] + _dot_tn(bm, xt)


def _ssd(xbc, dt_raw, dt_bias, a_log, d_skip, tri, e_f, e_b):
    bsz, length, width = xbc.shape
    n_chunks = length // CHUNK
    d_ssd = SSD_HEADS * SSD_HEADDIM
    fwd = lambda b, i: (b, i, 0)
    bwd = lambda b, i: (b, n_chunks - 1 - i, 0)
    in_specs = [
        pl.BlockSpec(memory_space=pltpu.SMEM),
        pl.BlockSpec((1, CHUNK, width), fwd),
        pl.BlockSpec((1, CHUNK, width), bwd),
        pl.BlockSpec((1, CHUNK, LANES), fwd),
        pl.BlockSpec((1, CHUNK, LANES), bwd),
        _const_spec(dt_bias.shape), _const_spec(a_log.shape), _const_spec(tri.shape),
        _const_spec(e_f.shape), _const_spec(e_b.shape),
    ]
    out_shape = (jax.ShapeDtypeStruct((bsz, length, d_ssd), BF16),
                 jax.ShapeDtypeStruct((bsz, length, d_ssd), BF16))
    out_specs = (pl.BlockSpec((1, CHUNK, d_ssd), fwd), pl.BlockSpec((1, CHUNK, d_ssd), bwd))
    state = pltpu.VMEM((SSD_GROUPS, SSD_STATE, GROUP_WIDTH), F32)
    return pl.pallas_call(
        _ssd_kernel, grid=(bsz, n_chunks), in_specs=in_specs, out_specs=out_specs,
        out_shape=out_shape, scratch_shapes=[state, state],
        compiler_params=pltpu.CompilerParams(
            dimension_semantics=("parallel", "arbitrary"), vmem_limit_bytes=VMEM_LIMIT_BYTES),
        name="ssd",
    )(d_skip, xbc, xbc, dt_raw, dt_raw, dt_bias, a_log, tri, e_f, e_b)


def _mixout_kernel(h_ref, yf_ref, yb_ref, z_ref, ysc_ref, gw_ref, wo_ref, out_ref):
    d_ssd = yf_ref.shape[-1]
    y = (yf_ref[0].astype(F32) + yb_ref[0].astype(F32)) * _silu(z_ref[0].astype(F32))
    normed = []
    for g in range(SSD_GROUPS):
        gs = slice(g * GROUP_WIDTH, (g + 1) * GROUP_WIDTH)
        normed.append(_rms(y[:, gs], gw_ref[:, gs]).astype(BF16))
    yn = jnp.concatenate(normed, axis=1)
    out = _dot(yn, wo_ref[0:d_ssd, :]) + _dot(ysc_ref[0], wo_ref[d_ssd:, :])
    out_ref[0] = h_ref[0] + out


def _mixout(h, y_f, y_b, z, y_sc, gnorm_w, w_out, *, tile):
    bsz, length, d = h.shape
    row_spec = lambda width: pl.BlockSpec((1, tile, width), lambda b, t: (b, t, 0))
    in_specs = [row_spec(d), row_spec(y_f.shape[-1]), row_spec(y_b.shape[-1]),
                row_spec(z.shape[-1]), row_spec(y_sc.shape[-1]),
                _const_spec(gnorm_w.shape), _const_spec(w_out.shape)]
    return pl.pallas_call(
        _mixout_kernel, grid=(bsz, length // tile), in_specs=in_specs, out_specs=row_spec(d),
        out_shape=jax.ShapeDtypeStruct(h.shape, F32),
        compiler_params=pltpu.CompilerParams(
            dimension_semantics=("parallel", "parallel"), vmem_limit_bytes=VMEM_LIMIT_BYTES),
        name="mixout",
    )(h, y_f, y_b, z, y_sc, gnorm_w, w_out)


FF_COL_BLOCK = 256


def _ffn_kernel(hp_ref, h_ref, hn_ref, nw_ref, wup_ref, cw_ref, cb_ref, wdn_ref, fw_ref,
                out_ref, lhs_ref, act_ref, *, final_norm):
    rows = h_ref.shape[1]
    d_ff = wdn_ref.shape[0]
    _halo_lhs(hp_ref, h_ref, hn_ref, nw_ref[...], lhs_ref)
    full = lhs_ref[...]
    for c in range(0, d_ff, FF_COL_BLOCK):
        gs = slice(c, c + FF_COL_BLOCK)
        us = slice(d_ff + c, d_ff + c + FF_COL_BLOCK)
        gate = _conv3(_dot(full, wup_ref[:, gs]), cw_ref.at[:, gs], rows) + cb_ref[:, gs]
        up = _conv3(_dot(full, wup_ref[:, us]), cw_ref.at[:, us], rows) + cb_ref[:, us]
        act_ref[:, gs] = (_silu(gate) * up).astype(BF16)
    out = h_ref[0] + _dot(act_ref[...], wdn_ref[...])
    if final_norm:
        out = _rms(out, fw_ref[...])
    out_ref[0] = out


def _ffn(h, norm_w, w_up, conv_w, conv_b, w_down, final_w, *, tile, final_norm):
    bsz, length, d = h.shape
    d_ff = w_down.shape[0]
    blk8 = tile // F32_SUBLANES
    last8 = length // F32_SUBLANES - 1
    row_spec = pl.BlockSpec((1, tile, d), lambda b, t: (b, t, 0))
    in_specs = [
        pl.BlockSpec((1, F32_SUBLANES, d), lambda b, t: (b, jnp.maximum(t * blk8 - 1, 0), 0)),
        row_spec,
        pl.BlockSpec((1, F32_SUBLANES, d), lambda b, t: (b, jnp.minimum((t + 1) * blk8, last8), 0)),
        _const_spec(norm_w.shape), _const_spec(w_up.shape), _const_spec(conv_w.shape),
        _const_spec(conv_b.shape), _const_spec(w_down.shape), _const_spec(final_w.shape),
    ]
    return pl.pallas_call(
        functools.partial(_ffn_kernel, final_norm=final_norm),
        grid=(bsz, length // tile), in_specs=in_specs, out_specs=row_spec,
        out_shape=jax.ShapeDtypeStruct(h.shape, F32),
        scratch_shapes=[pltpu.VMEM((tile + 2 * HALO, d), BF16), pltpu.VMEM((tile, d_ff), BF16)],
        compiler_params=pltpu.CompilerParams(
            dimension_semantics=("parallel", "parallel"), vmem_limit_bytes=VMEM_LIMIT_BYTES),
        name="ffn",
    )(h, h, h, norm_w, w_up, conv_w, conv_b, w_down, final_w)


def _pad_lanes(v):
    return jnp.pad(v.astype(F32), (0, LANES - v.shape[0])).reshape(1, LANES)


def _head_expanders():
    k = jnp.arange(LANES)[:, None]
    c = jnp.arange(SSD_HEADS * SSD_HEADDIM)[None, :] // SSD_HEADDIM
    return (k == c).astype(BF16), (k == c + SSD_HEADS).astype(BF16)


def kernel(x, w_in, conv_xbc_w, conv_xbc_b, dt_bias_f, dt_bias_b, a_log_f, a_log_b, d_skip,
           ssd_norm_w, sc_conv_w, w_out, norm1_w, norm2_w, w_ffn_up, ffn_conv_w, ffn_conv_b,
           w_ffn_down, final_norm_w):
    depth = w_in.shape[0]
    d_ssd = SSD_HEADS * SSD_HEADDIM
    conv_xbc = d_ssd + 2 * SSD_GROUPS * SSD_STATE
    d_sc = sc_conv_w.shape[-1]
    o_xbc = d_ssd
    o_dt = o_xbc + conv_xbc
    o_scb = o_dt + 2 * SSD_HEADS
    o_scc = o_scb + d_sc
    o_scv = o_scc + d_sc
    length = x.shape[1]
    tile = min(512, length)
    tri = (jnp.arange(CHUNK)[:, None] >= jnp.arange(CHUNK)[None, :]).astype(BF16)
    e_f, e_b = _head_expanders()
    row = lambda v: v.astype(F32).reshape(1, -1)

    h = x
    for l in range(depth):
        w = w_in[l]
        wz = w[:, :o_xbc].astype(BF16)
        wxbc = w[:, o_xbc:o_dt].astype(BF16)
        wdt = jnp.pad(w[:, o_dt:o_scb], ((0, 0), (0, LANES - 2 * SSD_HEADS))).astype(BF16)
        wscb = w[:, o_scb:o_scc].astype(BF16)
        wscc = w[:, o_scc:o_scv].astype(BF16)
        wscv = w[:, o_scv:].astype(BF16)
        z, xbc, y_sc, dt_raw = _inproj(
            h, row(norm1_w[l]), wz, wxbc, wscb, wscc, wscv, wdt,
            conv_xbc_w[l].astype(F32), row(conv_xbc_b[l]), sc_conv_w[l].astype(F32), tile=tile)
        y_f, y_b = _ssd(
            xbc, dt_raw,
            _pad_lanes(jnp.concatenate([dt_bias_f[l], dt_bias_b[l]])),
            _pad_lanes(jnp.concatenate([a_log_f[l], a_log_b[l]])),
            d_skip[l].astype(F32), tri, e_f, e_b)
        h = _mixout(h, y_f, y_b, z, y_sc, row(ssd_norm_w[l]), w_out[l].astype(BF16), tile=tile)
        h = _ffn(h, row(norm2_w[l]), w_ffn_up[l].astype(BF16), ffn_conv_w[l].astype(F32),
                 row(ffn_conv_b[l]), w_ffn_down[l].astype(BF16), row(final_norm_w),
                 tile=tile, final_norm=(l == depth - 1))
    return h
```

```python
import functools

import jax
import jax.numpy as jnp
from jax import lax
from jax.experimental import pallas as pl
from jax.experimental.pallas import tpu as pltpu

F32 = jnp.float32
BF16 = jnp.bfloat16

RMS_EPS = 1e-5
SSD_HEADDIM = 64
SSD_GROUPS = 4
SSD_STATE = 128
CHUNK = 128
HEADS_PER_GROUP = 4
SSD_HEADS = SSD_GROUPS * HEADS_PER_GROUP
GROUP_WIDTH = HEADS_PER_GROUP * SSD_HEADDIM

LANES = 128
F32_SUBLANES = 8
BF16_SUBLANES = 16
HALO = BF16_SUBLANES
VMEM_LIMIT_BYTES = 56 * 1024 * 1024


def _dot(a, b):
    return jnp.dot(a, b, preferred_element_type=F32)


def _dot_nt(a, b):
    return lax.dot_general(a, b, (((1,), (1,)), ((), ())), preferred_element_type=F32)


def _dot_tn(a, b):
    return lax.dot_general(a, b, (((0,), (0,)), ((), ())), preferred_element_type=F32)


def _silu(x):
    return x * (1.0 / (1.0 + jnp.exp(-x)))


def _rms(v, w):
    ms = jnp.mean(v * v, axis=-1, keepdims=True)
    return v * lax.rsqrt(ms + RMS_EPS) * w


def _const_spec(shape):
    zeros = (0,) * len(shape)
    return pl.BlockSpec(shape, lambda *_: zeros, pipeline_mode=pl.Buffered(1))


def _halo_lhs(prev_ref, main_ref, next_ref, norm_w, lhs_ref):
    t = pl.program_id(1)
    last = pl.num_programs(1) - 1
    d = main_ref.shape[-1]
    prev = _rms(prev_ref[0], norm_w) * (t > 0).astype(F32)
    nxt = _rms(next_ref[0], norm_w) * (t < last).astype(F32)
    main = _rms(main_ref[0], norm_w)
    zeros = jnp.zeros((HALO - F32_SUBLANES, d), F32)
    lhs_ref[...] = jnp.concatenate([zeros, prev, main, nxt, zeros], axis=0).astype(BF16)


def _conv3(p_ref, w_ref, rows):
    return (p_ref[HALO - 1:HALO - 1 + rows, :] * w_ref[0:1, :]
            + p_ref[HALO:HALO + rows, :] * w_ref[1:2, :]
            + p_ref[HALO + 1:HALO + 1 + rows, :] * w_ref[2:3, :])


IN_COL_BLOCK = 512


def _inproj_kernel(hp_ref, h_ref, hn_ref, nw_ref, wz_ref, wxbc_ref, wscb_ref, wscc_ref,
                   wscv_ref, wdt_ref, cw_ref, cb_ref, scw_ref,
                   z_out, xbc_out, ysc_out, dt_out, lhs_ref, p_ref):
    rows = h_ref.shape[1]
    _halo_lhs(hp_ref, h_ref, hn_ref, nw_ref[...], lhs_ref)
    centre = lhs_ref[HALO:HALO + rows, :]
    full = lhs_ref[...]

    dt_out[0] = _dot(centre, wdt_ref[...])
    for c in range(0, wz_ref.shape[1], IN_COL_BLOCK):
        cs = slice(c, c + IN_COL_BLOCK)
        z_out[0, :, cs] = _dot(centre, wz_ref[:, cs]).astype(z_out.dtype)
    slot = 0
    for c in range(0, wxbc_ref.shape[1], IN_COL_BLOCK):
        cs = slice(c, c + IN_COL_BLOCK)
        pbuf = p_ref.at[slot]
        slot = 1 - slot
        pbuf[...] = _dot(full, wxbc_ref[:, cs])
        conv = _conv3(pbuf, cw_ref.at[:, cs], rows) + cb_ref[:, cs]
        xbc_out[0, :, cs] = _silu(conv).astype(xbc_out.dtype)
    for c in range(0, wscb_ref.shape[1], IN_COL_BLOCK):
        cs = slice(c, c + IN_COL_BLOCK)
        pbuf = p_ref.at[slot]
        slot = 1 - slot
        pbuf[...] = _dot(full, wscc_ref[:, cs]) * _dot(full, wscv_ref[:, cs])
        gate = _dot(centre, wscb_ref[:, cs])
        ysc_out[0, :, cs] = (gate * _conv3(pbuf, scw_ref.at[:, cs], rows)).astype(ysc_out.dtype)


def _inproj(h, norm_w, wz, wxbc, wscb, wscc, wscv, wdt, conv_w, conv_b, sc_w, *, tile):
    bsz, length, d = h.shape
    n_tiles = length // tile
    blk8 = tile // F32_SUBLANES
    last8 = length // F32_SUBLANES - 1
    grid = (bsz, n_tiles)
    row_spec = lambda width: pl.BlockSpec((1, tile, width), lambda b, t: (b, t, 0))
    in_specs = [
        pl.BlockSpec((1, F32_SUBLANES, d), lambda b, t: (b, jnp.maximum(t * blk8 - 1, 0), 0)),
        row_spec(d),
        pl.BlockSpec((1, F32_SUBLANES, d), lambda b, t: (b, jnp.minimum((t + 1) * blk8, last8), 0)),
        _const_spec(norm_w.shape), _const_spec(wz.shape), _const_spec(wxbc.shape),
        _const_spec(wscb.shape), _const_spec(wscc.shape), _const_spec(wscv.shape),
        _const_spec(wdt.shape), _const_spec(conv_w.shape), _const_spec(conv_b.shape),
        _const_spec(sc_w.shape),
    ]
    out_shape = (
        jax.ShapeDtypeStruct((bsz, length, wz.shape[1]), BF16),
        jax.ShapeDtypeStruct((bsz, length, wxbc.shape[1]), BF16),
        jax.ShapeDtypeStruct((bsz, length, wscb.shape[1]), BF16),
        jax.ShapeDtypeStruct((bsz, length, LANES), F32),
    )
    out_specs = (row_spec(wz.shape[1]), row_spec(wxbc.shape[1]), row_spec(wscb.shape[1]),
                 row_spec(LANES))
    return pl.pallas_call(
        _inproj_kernel, grid=grid, in_specs=in_specs, out_specs=out_specs, out_shape=out_shape,
        scratch_shapes=[pltpu.VMEM((tile + 2 * HALO, d), BF16),
                        pltpu.VMEM((2, tile + 2 * HALO, IN_COL_BLOCK), F32)],
        compiler_params=pltpu.CompilerParams(
            dimension_semantics=("parallel", "parallel"), vmem_limit_bytes=VMEM_LIMIT_BYTES),
        name="inproj",
    )(h, h, h, norm_w, wz, wxbc, wscb, wscc, wscv, wdt, conv_w, conv_b, sc_w)


N_EXPAND = 4


def _softplus(x):
    return jnp.maximum(x, 0.0) + jnp.log(1.0 + jnp.exp(-jnp.abs(x)))


def _cumsum_lanes(a):
    lane = lax.broadcasted_iota(jnp.int32, a.shape, 1)
    shift = 1
    while shift < a.shape[1]:
        a = a + jnp.where(lane >= shift, pltpu.roll(a, shift, 1), 0.0)
        shift *= 2
    return a


def _pad_rows(v, rows):
    return jnp.concatenate([v, jnp.zeros((rows - v.shape[0], v.shape[1]), v.dtype)], axis=0)


PREP_CHUNKS = 8


def _ssd_prep_kernel(dt_ref, bias_ref, alog_ref, rows_out, cols_out, packed_out):
    nh = SSD_HEADS
    raw = jnp.concatenate(
        [dt_ref[0, c * CHUNK:(c + 1) * CHUNK, :].T[0:2 * nh] for c in range(PREP_CHUNKS)], axis=0)
    bias = jnp.concatenate([bias_ref[...]] * PREP_CHUNKS, axis=0)
    neg_a = -jnp.exp(jnp.concatenate([alog_ref[...]] * PREP_CHUNKS, axis=0))
    dt = _softplus(raw + bias)
    a = dt * neg_a
    acs = _cumsum_lanes(a)
    ecs = acs - a
    tot = jnp.broadcast_to(acs[:, CHUNK - 1:CHUNK], acs.shape)
    log_dt = jnp.log(dt)
    grow_f = jnp.exp(acs)
    xscale_f = dt * jnp.exp(tot - acs)
    grow_b = jnp.exp(tot - ecs)
    xscale_b = dt * jnp.exp(ecs)
    for c in range(PREP_CHUNKS):
        fwd = slice(2 * nh * c, 2 * nh * c + nh)
        bwd = slice(2 * nh * c + nh, 2 * nh * (c + 1))
        rows_out[0, c] = jnp.concatenate(
            [acs[fwd] - log_dt[fwd],
             ecs[bwd] + log_dt[bwd],
             jnp.log(dt[fwd] + dt[bwd])], axis=0)
        tok = slice(c * CHUNK, (c + 1) * CHUNK)
        cols_out[0, tok, :] = _pad_rows(jnp.concatenate([acs[fwd], ecs[bwd]], axis=0), CHUNK).T
        wide = jnp.concatenate([grow_f[fwd], xscale_f[fwd], grow_b[bwd], xscale_b[bwd]], axis=0)
        wide_hi = wide.astype(BF16).astype(F32)
        packed_out[0, tok, :] = jnp.concatenate([wide_hi, wide - wide_hi], axis=0).T.astype(BF16)


def _ssd_prep(dt_raw, dt_bias, a_log):
    bsz, length, _ = dt_raw.shape
    n_chunks = length // CHUNK
    span = PREP_CHUNKS * CHUNK
    tok_spec = pl.BlockSpec((1, span, LANES), lambda b, t: (b, t, 0))
    out_shape = (jax.ShapeDtypeStruct((bsz, n_chunks, 3 * SSD_HEADS, CHUNK), F32),
                 jax.ShapeDtypeStruct((bsz, length, LANES), F32),
                 jax.ShapeDtypeStruct((bsz, length, LANES), BF16))
    out_specs = (pl.BlockSpec((1, PREP_CHUNKS, 3 * SSD_HEADS, CHUNK), lambda b, t: (b, t, 0, 0)),
                 tok_spec, tok_spec)
    return pl.pallas_call(
        _ssd_prep_kernel, grid=(bsz, length // span),
        in_specs=[tok_spec, _const_spec(dt_bias.shape), _const_spec(a_log.shape)],
        out_specs=out_specs, out_shape=out_shape,
        compiler_params=pltpu.CompilerParams(
            dimension_semantics=("parallel", "parallel"), vmem_limit_bytes=VMEM_LIMIT_BYTES),
        name="ssd_prep",
    )(dt_raw, dt_bias, a_log)


def _ssd_kernel(xf_ref, xb_ref, rows_ref, cols_ref, pf_ref, pb_ref, dexp_ref, e_ref,
                yf_ref, yb_ref, hf_ref, hb_ref):
    d_ssd = SSD_HEADS * SSD_HEADDIM
    b_off = d_ssd
    c_off = d_ssd + SSD_GROUPS * SSD_STATE
    nh = SSD_HEADS

    @pl.when(pl.program_id(1) == 0)
    def _():
        hf_ref[...] = jnp.zeros_like(hf_ref)
        hb_ref[...] = jnp.zeros_like(hb_ref)

    row_f = rows_ref[0, 0, 0:nh, :]
    row_b = rows_ref[0, 0, nh:2 * nh, :]
    row_d = rows_ref[0, 0, 2 * nh:3 * nh, :]
    cols = cols_ref[0]
    packed_f = pf_ref[0]
    packed_b = pb_ref[0]
    grow_f = _dot(packed_f, e_ref[0])
    xscale_f = _dot(packed_f, e_ref[1])
    grow_b = _dot(packed_b, e_ref[2])
    xscale_b = _dot(packed_b, e_ref[3])
    hdecay_f = grow_f[CHUNK - 1:CHUNK, :]
    hdecay_b = grow_b[0:1, :]

    row_i = lax.broadcasted_iota(jnp.int32, (CHUNK, CHUNK), 0)
    col_i = lax.broadcasted_iota(jnp.int32, (CHUNK, CHUNK), 1)
    below = row_i > col_i
    above = row_i < col_i
    head_lane = lax.broadcasted_iota(jnp.int32, (CHUNK, GROUP_WIDTH), 1) // SSD_HEADDIM

    for g in range(SSD_GROUPS):
        gs = slice(g * GROUP_WIDTH, (g + 1) * GROUP_WIDTH)
        bs = slice(b_off + g * SSD_STATE, b_off + (g + 1) * SSD_STATE)
        cs = slice(c_off + g * SSD_STATE, c_off + (g + 1) * SSD_STATE)

        xs = xf_ref[0, :, gs]
        bm = xf_ref[0, :, bs]
        cm = xf_ref[0, :, cs]
        scores = _dot_nt(cm, bm)
        w_heads = []
        x_blocks = []
        for r in range(HEADS_PER_GROUP):
            hd = g * HEADS_PER_GROUP + r
            expo = jnp.where(below, cols[:, hd:hd + 1] - row_f[hd:hd + 1, :],
                             jnp.where(above, row_b[hd:hd + 1, :] - cols[:, nh + hd:nh + hd + 1],
                                       row_d[hd:hd + 1, :]))
            w_heads.append((scores * jnp.exp(expo)).astype(BF16))
            x_blocks.append(jnp.where(head_lane == r, xs, jnp.zeros_like(xs)))
        xs32 = xs.astype(F32)
        y = _dot(jnp.concatenate(w_heads, axis=1), jnp.concatenate(x_blocks, axis=0))
        y = y + xs32 * dexp_ref[:, gs] + _dot(cm, hf_ref[g].astype(BF16)) * grow_f[:, gs]
        yf_ref[0, :, gs] = y.astype(yf_ref.dtype)
        xt = (xs32 * xscale_f[:, gs]).astype(BF16)
        hf_ref[g] = hf_ref[g] * hdecay_f[:, gs] + _dot_tn(bm, xt)

        xs = xb_ref[0, :, gs]
        bm = xb_ref[0, :, bs]
        cm = xb_ref[0, :, cs]
        yb_ref[0, :, gs] = (_dot(cm, hb_ref[g].astype(BF16)) * grow_b[:, gs]).astype(yb_ref.dtype)
        xt = (xs.astype(F32) * xscale_b[:, gs]).astype(BF16)
        hb_ref[g] = hb_ref[g] * hdecay_b[:, gs] + _dot_tn(bm, xt)


def _ssd(xbc, rows, cols, packed, d_expand, expanders):
    bsz, length, width = xbc.shape
    n_chunks = length // CHUNK
    d_ssd = SSD_HEADS * SSD_HEADDIM
    fwd = lambda b, i: (b, i, 0)
    bwd = lambda b, i: (b, n_chunks - 1 - i, 0)
    in_specs = [
        pl.BlockSpec((1, CHUNK, width), fwd),
        pl.BlockSpec((1, CHUNK, width), bwd),
        pl.BlockSpec((1, 1, 3 * SSD_HEADS, CHUNK), lambda b, i: (b, i, 0, 0)),
        pl.BlockSpec((1, CHUNK, LANES), fwd),
        pl.BlockSpec((1, CHUNK, LANES), fwd),
        pl.BlockSpec((1, CHUNK, LANES), bwd),
        _const_spec(d_expand.shape), _const_spec(expanders.shape),
    ]
    out_shape = (jax.ShapeDtypeStruct((bsz, length, d_ssd), BF16),
                 jax.ShapeDtypeStruct((bsz, length, d_ssd), BF16))
    out_specs = (pl.BlockSpec((1, CHUNK, d_ssd), fwd), pl.BlockSpec((1, CHUNK, d_ssd), bwd))
    state = pltpu.VMEM((SSD_GROUPS, SSD_STATE, GROUP_WIDTH), F32)
    return pl.pallas_call(
        _ssd_kernel, grid=(bsz, n_chunks), in_specs=in_specs, out_specs=out_specs,
        out_shape=out_shape, scratch_shapes=[state, state],
        compiler_params=pltpu.CompilerParams(
            dimension_semantics=("parallel", "arbitrary"), vmem_limit_bytes=VMEM_LIMIT_BYTES),
        name="ssd",
    )(xbc, xbc, rows, cols, packed, packed, d_expand, expanders)


def _mixout_kernel(h_ref, yf_ref, yb_ref, z_ref, ysc_ref, gw_ref, wo_ref, out_ref):
    d_ssd = yf_ref.shape[-1]
    y = (yf_ref[0].astype(F32) + yb_ref[0].astype(F32)) * _silu(z_ref[0].astype(F32))
    normed = []
    for g in range(SSD_GROUPS):
        gs = slice(g * GROUP_WIDTH, (g + 1) * GROUP_WIDTH)
        normed.append(_rms(y[:, gs], gw_ref[:, gs]).astype(BF16))
    yn = jnp.concatenate(normed, axis=1)
    out = _dot(yn, wo_ref[0:d_ssd, :]) + _dot(ysc_ref[0], wo_ref[d_ssd:, :])
    out_ref[0] = h_ref[0] + out


def _mixout(h, y_f, y_b, z, y_sc, gnorm_w, w_out, *, tile):
    bsz, length, d = h.shape
    row_spec = lambda width: pl.BlockSpec((1, tile, width), lambda b, t: (b, t, 0))
    in_specs = [row_spec(d), row_spec(y_f.shape[-1]), row_spec(y_b.shape[-1]),
                row_spec(z.shape[-1]), row_spec(y_sc.shape[-1]),
                _const_spec(gnorm_w.shape), _const_spec(w_out.shape)]
    return pl.pallas_call(
        _mixout_kernel, grid=(bsz, length // tile), in_specs=in_specs, out_specs=row_spec(d),
        out_shape=jax.ShapeDtypeStruct(h.shape, F32),
        compiler_params=pltpu.CompilerParams(
            dimension_semantics=("parallel", "parallel"), vmem_limit_bytes=VMEM_LIMIT_BYTES),
        name="mixout",
    )(h, y_f, y_b, z, y_sc, gnorm_w, w_out)


FF_COL_BLOCK = 256


def _ffn_kernel(hp_ref, h_ref, hn_ref, nw_ref, wup_ref, cw_ref, cb_ref, wdn_ref, fw_ref,
                out_ref, lhs_ref, act_ref, p_ref, *, final_norm):
    rows = h_ref.shape[1]
    d_ff = wdn_ref.shape[0]
    _halo_lhs(hp_ref, h_ref, hn_ref, nw_ref[...], lhs_ref)
    full = lhs_ref[...]
    slot = 0
    for c in range(0, d_ff, FF_COL_BLOCK):
        gs = slice(c, c + FF_COL_BLOCK)
        us = slice(d_ff + c, d_ff + c + FF_COL_BLOCK)
        gbuf = p_ref.at[slot, 0]
        ubuf = p_ref.at[slot, 1]
        slot = 1 - slot
        gbuf[...] = _dot(full, wup_ref[:, gs])
        ubuf[...] = _dot(full, wup_ref[:, us])
        gate = _conv3(gbuf, cw_ref.at[:, gs], rows) + cb_ref[:, gs]
        up = _conv3(ubuf, cw_ref.at[:, us], rows) + cb_ref[:, us]
        act_ref[:, gs] = (_silu(gate) * up).astype(BF16)
    out = h_ref[0] + _dot(act_ref[...], wdn_ref[...])
    if final_norm:
        out = _rms(out, fw_ref[...])
    out_ref[0] = out


def _ffn(h, norm_w, w_up, conv_w, conv_b, w_down, final_w, *, tile, final_norm):
    bsz, length, d = h.shape
    d_ff = w_down.shape[0]
    blk8 = tile // F32_SUBLANES
    last8 = length // F32_SUBLANES - 1
    row_spec = pl.BlockSpec((1, tile, d), lambda b, t: (b, t, 0))
    in_specs = [
        pl.BlockSpec((1, F32_SUBLANES, d), lambda b, t: (b, jnp.maximum(t * blk8 - 1, 0), 0)),
        row_spec,
        pl.BlockSpec((1, F32_SUBLANES, d), lambda b, t: (b, jnp.minimum((t + 1) * blk8, last8), 0)),
        _const_spec(norm_w.shape), _const_spec(w_up.shape), _const_spec(conv_w.shape),
        _const_spec(conv_b.shape), _const_spec(w_down.shape), _const_spec(final_w.shape),
    ]
    return pl.pallas_call(
        functools.partial(_ffn_kernel, final_norm=final_norm),
        grid=(bsz, length // tile), in_specs=in_specs, out_specs=row_spec,
        out_shape=jax.ShapeDtypeStruct(h.shape, F32),
        scratch_shapes=[pltpu.VMEM((tile + 2 * HALO, d), BF16), pltpu.VMEM((tile, d_ff), BF16),
                        pltpu.VMEM((2, 2, tile + 2 * HALO, FF_COL_BLOCK), F32)],
        compiler_params=pltpu.CompilerParams(
            dimension_semantics=("parallel", "parallel"), vmem_limit_bytes=VMEM_LIMIT_BYTES),
        name="ffn",
    )(h, h, h, norm_w, w_up, conv_w, conv_b, w_down, final_w)


def _lane_rows(fwd, bwd):
    v = jnp.concatenate([fwd, bwd]).astype(F32)
    return jnp.broadcast_to(v[:, None], (v.shape[0], LANES))


def _head_expanders():
    j = jnp.arange(LANES)[None, :, None]
    k = jnp.arange(N_EXPAND)[:, None, None]
    head = jnp.arange(SSD_HEADS * SSD_HEADDIM)[None, None, :] // SSD_HEADDIM
    half = N_EXPAND * SSD_HEADS
    return ((j == k * SSD_HEADS + head) | (j == half + k * SSD_HEADS + head)).astype(BF16)


def kernel(x, w_in, conv_xbc_w, conv_xbc_b, dt_bias_f, dt_bias_b, a_log_f, a_log_b, d_skip,
           ssd_norm_w, sc_conv_w, w_out, norm1_w, norm2_w, w_ffn_up, ffn_conv_w, ffn_conv_b,
           w_ffn_down, final_norm_w):
    depth = w_in.shape[0]
    d_ssd = SSD_HEADS * SSD_HEADDIM
    conv_xbc = d_ssd + 2 * SSD_GROUPS * SSD_STATE
    d_sc = sc_conv_w.shape[-1]
    o_xbc = d_ssd
    o_dt = o_xbc + conv_xbc
    o_scb = o_dt + 2 * SSD_HEADS
    o_scc = o_scb + d_sc
    o_scv = o_scc + d_sc
    length = x.shape[1]
    tile = min(512, length)
    expanders = _head_expanders()
    row = lambda v: v.astype(F32).reshape(1, -1)

    h = x
    for l in range(depth):
        w = w_in[l]
        wz = w[:, :o_xbc].astype(BF16)
        wxbc = w[:, o_xbc:o_dt].astype(BF16)
        wdt = jnp.pad(w[:, o_dt:o_scb], ((0, 0), (0, LANES - 2 * SSD_HEADS))).astype(BF16)
        wscb = w[:, o_scb:o_scc].astype(BF16)
        wscc = w[:, o_scc:o_scv].astype(BF16)
        wscv = w[:, o_scv:].astype(BF16)
        z, xbc, y_sc, dt_raw = _inproj(
            h, row(norm1_w[l]), wz, wxbc, wscb, wscc, wscv, wdt,
            conv_xbc_w[l].astype(F32), row(conv_xbc_b[l]), sc_conv_w[l].astype(F32), tile=tile)
        rows, cols, packed = _ssd_prep(
            dt_raw, _lane_rows(dt_bias_f[l], dt_bias_b[l]), _lane_rows(a_log_f[l], a_log_b[l]))
        y_f, y_b = _ssd(xbc, rows, cols, packed,
                        jnp.repeat(d_skip[l].astype(F32), SSD_HEADDIM).reshape(1, -1), expanders)
        h = _mixout(h, y_f, y_b, z, y_sc, row(ssd_norm_w[l]), w_out[l].astype(BF16), tile=tile)
        h = _ffn(h, row(norm2_w[l]), w_ffn_up[l].astype(BF16), ffn_conv_w[l].astype(F32),
                 row(ffn_conv_b[l]), w_ffn_down[l].astype(BF16), row(final_norm_w),
                 tile=tile, final_norm=(l == depth - 1))
    return h
```

```python
import functools

import jax
import jax.numpy as jnp
from jax import lax
from jax.experimental import pallas as pl
from jax.experimental.pallas import tpu as pltpu

F32 = jnp.float32
BF16 = jnp.bfloat16

RMS_EPS = 1e-5
SSD_HEADDIM = 64
SSD_GROUPS = 4
SSD_STATE = 128
CHUNK = 128
HEADS_PER_GROUP = 4
SSD_HEADS = SSD_GROUPS * HEADS_PER_GROUP
GROUP_WIDTH = HEADS_PER_GROUP * SSD_HEADDIM

LANES = 128
F32_SUBLANES = 8
BF16_SUBLANES = 16
HALO = BF16_SUBLANES
VMEM_LIMIT_BYTES = 56 * 1024 * 1024


def _dot(a, b):
    return jnp.dot(a, b, preferred_element_type=F32)


def _dot_nt(a, b):
    return lax.dot_general(a, b, (((1,), (1,)), ((), ())), preferred_element_type=F32)


def _dot_tn(a, b):
    return lax.dot_general(a, b, (((0,), (0,)), ((), ())), preferred_element_type=F32)


def _silu(x):
    return x * (1.0 / (1.0 + jnp.exp(-x)))


def _rms(v, w):
    ms = jnp.mean(v * v, axis=-1, keepdims=True)
    return v * lax.rsqrt(ms + RMS_EPS) * w


def _const_spec(shape):
    zeros = (0,) * len(shape)
    return pl.BlockSpec(shape, lambda *_: zeros, pipeline_mode=pl.Buffered(1))


def _halo_lhs(prev_ref, main_ref, next_ref, norm_w, lhs_ref):
    t = pl.program_id(1)
    last = pl.num_programs(1) - 1
    d = main_ref.shape[-1]
    prev = _rms(prev_ref[0], norm_w) * (t > 0).astype(F32)
    nxt = _rms(next_ref[0], norm_w) * (t < last).astype(F32)
    main = _rms(main_ref[0], norm_w)
    zeros = jnp.zeros((HALO - F32_SUBLANES, d), F32)
    lhs_ref[...] = jnp.concatenate([zeros, prev, main, nxt, zeros], axis=0).astype(BF16)


def _conv3(p_ref, w_ref, rows):
    return (p_ref[HALO - 1:HALO - 1 + rows, :] * w_ref[0:1, :]
            + p_ref[HALO:HALO + rows, :] * w_ref[1:2, :]
            + p_ref[HALO + 1:HALO + 1 + rows, :] * w_ref[2:3, :])


IN_COL_BLOCK = 512


def _inproj_kernel(hp_ref, h_ref, hn_ref, nw_ref, wz_ref, wxbc_ref, wscb_ref, wscc_ref,
                   wscv_ref, wdt_ref, cw_ref, cb_ref, scw_ref,
                   z_out, xbc_out, ysc_out, dt_out, lhs_ref, p_ref):
    rows = h_ref.shape[1]
    _halo_lhs(hp_ref, h_ref, hn_ref, nw_ref[...], lhs_ref)
    centre = lhs_ref[HALO:HALO + rows, :]
    full = lhs_ref[...]

    dt_out[0] = _dot(centre, wdt_ref[...])
    for c in range(0, wz_ref.shape[1], IN_COL_BLOCK):
        cs = slice(c, c + IN_COL_BLOCK)
        z_out[0, :, cs] = _dot(centre, wz_ref[:, cs]).astype(z_out.dtype)
    slot = 0
    for c in range(0, wxbc_ref.shape[1], IN_COL_BLOCK):
        cs = slice(c, c + IN_COL_BLOCK)
        pbuf = p_ref.at[slot]
        slot = 1 - slot
        pbuf[...] = _dot(full, wxbc_ref[:, cs])
        conv = _conv3(pbuf, cw_ref.at[:, cs], rows) + cb_ref[:, cs]
        xbc_out[0, :, cs] = _silu(conv).astype(xbc_out.dtype)
    for c in range(0, wscb_ref.shape[1], IN_COL_BLOCK):
        cs = slice(c, c + IN_COL_BLOCK)
        pbuf = p_ref.at[slot]
        slot = 1 - slot
        pbuf[...] = _dot(full, wscc_ref[:, cs]) * _dot(full, wscv_ref[:, cs])
        gate = _dot(centre, wscb_ref[:, cs])
        ysc_out[0, :, cs] = (gate * _conv3(pbuf, scw_ref.at[:, cs], rows)).astype(ysc_out.dtype)


def _inproj(h, norm_w, wz, wxbc, wscb, wscc, wscv, wdt, conv_w, conv_b, sc_w, *, tile):
    bsz, length, d = h.shape
    n_tiles = length // tile
    blk8 = tile // F32_SUBLANES
    last8 = length // F32_SUBLANES - 1
    grid = (bsz, n_tiles)
    row_spec = lambda width: pl.BlockSpec((1, tile, width), lambda b, t: (b, t, 0))
    in_specs = [
        pl.BlockSpec((1, F32_SUBLANES, d), lambda b, t: (b, jnp.maximum(t * blk8 - 1, 0), 0)),
        row_spec(d),
        pl.BlockSpec((1, F32_SUBLANES, d), lambda b, t: (b, jnp.minimum((t + 1) * blk8, last8), 0)),
        _const_spec(norm_w.shape), _const_spec(wz.shape), _const_spec(wxbc.shape),
        _const_spec(wscb.shape), _const_spec(wscc.shape), _const_spec(wscv.shape),
        _const_spec(wdt.shape), _const_spec(conv_w.shape), _const_spec(conv_b.shape),
        _const_spec(sc_w.shape),
    ]
    out_shape = (
        jax.ShapeDtypeStruct((bsz, length, wz.shape[1]), BF16),
        jax.ShapeDtypeStruct((bsz, length, wxbc.shape[1]), BF16),
        jax.ShapeDtypeStruct((bsz, length, wscb.shape[1]), BF16),
        jax.ShapeDtypeStruct((bsz, length, LANES), F32),
    )
    out_specs = (row_spec(wz.shape[1]), row_spec(wxbc.shape[1]), row_spec(wscb.shape[1]),
                 row_spec(LANES))
    return pl.pallas_call(
        _inproj_kernel, grid=grid, in_specs=in_specs, out_specs=out_specs, out_shape=out_shape,
        scratch_shapes=[pltpu.VMEM((tile + 2 * HALO, d), BF16),
                        pltpu.VMEM((2, tile + 2 * HALO, IN_COL_BLOCK), F32)],
        compiler_params=pltpu.CompilerParams(
            dimension_semantics=("parallel", "parallel"), vmem_limit_bytes=VMEM_LIMIT_BYTES),
        name="inproj",
    )(h, h, h, norm_w, wz, wxbc, wscb, wscc, wscv, wdt, conv_w, conv_b, sc_w)


N_EXPAND = 4


def _softplus(x):
    return jnp.maximum(x, 0.0) + jnp.log(1.0 + jnp.exp(-jnp.abs(x)))


def _cumsum_lanes(a):
    lane = lax.broadcasted_iota(jnp.int32, a.shape, 1)
    shift = 1
    while shift < a.shape[1]:
        a = a + jnp.where(lane >= shift, pltpu.roll(a, shift, 1), 0.0)
        shift *= 2
    return a


def _pad_rows(v, rows):
    return jnp.concatenate([v, jnp.zeros((rows - v.shape[0], v.shape[1]), v.dtype)], axis=0)


PREP_CHUNKS = 8


def _ssd_prep_kernel(dt_ref, bias_ref, alog_ref, rows_out, cols_out, packed_out):
    nh = SSD_HEADS
    raw = jnp.concatenate(
        [dt_ref[0, c * CHUNK:(c + 1) * CHUNK, :].T[0:2 * nh] for c in range(PREP_CHUNKS)], axis=0)
    bias = jnp.concatenate([bias_ref[...]] * PREP_CHUNKS, axis=0)
    neg_a = -jnp.exp(jnp.concatenate([alog_ref[...]] * PREP_CHUNKS, axis=0))
    dt = _softplus(raw + bias)
    a = dt * neg_a
    acs = _cumsum_lanes(a)
    ecs = acs - a
    tot = jnp.broadcast_to(acs[:, CHUNK - 1:CHUNK], acs.shape)
    log_dt = jnp.log(dt)
    grow_f = jnp.exp(acs)
    xscale_f = dt * jnp.exp(tot - acs)
    grow_b = jnp.exp(tot - ecs)
    xscale_b = dt * jnp.exp(ecs)
    for c in range(PREP_CHUNKS):
        fwd = slice(2 * nh * c, 2 * nh * c + nh)
        bwd = slice(2 * nh * c + nh, 2 * nh * (c + 1))
        rows_out[0, c] = jnp.concatenate(
            [acs[fwd] - log_dt[fwd],
             ecs[bwd] + log_dt[bwd],
             jnp.log(dt[fwd] + dt[bwd])], axis=0)
        tok = slice(c * CHUNK, (c + 1) * CHUNK)
        cols_out[0, tok, :] = _pad_rows(jnp.concatenate([acs[fwd], ecs[bwd]], axis=0), CHUNK).T
        wide = jnp.concatenate([grow_f[fwd], xscale_f[fwd], grow_b[bwd], xscale_b[bwd]], axis=0)
        wide_hi = wide.astype(BF16).astype(F32)
        packed_out[0, tok, :] = jnp.concatenate([wide_hi, wide - wide_hi], axis=0).T.astype(BF16)


def _ssd_prep(dt_raw, dt_bias, a_log):
    bsz, length, _ = dt_raw.shape
    n_chunks = length // CHUNK
    span = PREP_CHUNKS * CHUNK
    tok_spec = pl.BlockSpec((1, span, LANES), lambda b, t: (b, t, 0))
    out_shape = (jax.ShapeDtypeStruct((bsz, n_chunks, 3 * SSD_HEADS, CHUNK), F32),
                 jax.ShapeDtypeStruct((bsz, length, LANES), F32),
                 jax.ShapeDtypeStruct((bsz, length, LANES), BF16))
    out_specs = (pl.BlockSpec((1, PREP_CHUNKS, 3 * SSD_HEADS, CHUNK), lambda b, t: (b, t, 0, 0)),
                 tok_spec, tok_spec)
    return pl.pallas_call(
        _ssd_prep_kernel, grid=(bsz, length // span),
        in_specs=[tok_spec, _const_spec(dt_bias.shape), _const_spec(a_log.shape)],
        out_specs=out_specs, out_shape=out_shape,
        compiler_params=pltpu.CompilerParams(
            dimension_semantics=("parallel", "parallel"), vmem_limit_bytes=VMEM_LIMIT_BYTES),
        name="ssd_prep",
    )(dt_raw, dt_bias, a_log)


SSD_STEP_CHUNKS = 4


def _ssd_kernel(xf_ref, xb_ref, rows_ref, cols_ref, pf_ref, pb_ref, dexp_ref, e_ref,
                yf_ref, yb_ref, hf_ref, hb_ref):
    d_ssd = SSD_HEADS * SSD_HEADDIM
    b_off = d_ssd
    c_off = d_ssd + SSD_GROUPS * SSD_STATE
    nh = SSD_HEADS

    @pl.when(pl.program_id(1) == 0)
    def _():
        hf_ref[...] = jnp.zeros_like(hf_ref)
        hb_ref[...] = jnp.zeros_like(hb_ref)

    row_i = lax.broadcasted_iota(jnp.int32, (CHUNK, CHUNK), 0)
    col_i = lax.broadcasted_iota(jnp.int32, (CHUNK, CHUNK), 1)
    below = row_i > col_i
    above = row_i < col_i
    head_lane = lax.broadcasted_iota(jnp.int32, (CHUNK, GROUP_WIDTH), 1) // SSD_HEADDIM

    for k in range(SSD_STEP_CHUNKS):
        tf = slice(k * CHUNK, (k + 1) * CHUNK)
        tb = slice((SSD_STEP_CHUNKS - 1 - k) * CHUNK, (SSD_STEP_CHUNKS - k) * CHUNK)
        row_f = rows_ref[0, k, 0:nh, :]
        row_b = rows_ref[0, k, nh:2 * nh, :]
        row_d = rows_ref[0, k, 2 * nh:3 * nh, :]
        cols = cols_ref[0, tf, :]
        packed_f = pf_ref[0, tf, :]
        packed_b = pb_ref[0, tb, :]
        grow_f = _dot(packed_f, e_ref[0])
        xscale_f = _dot(packed_f, e_ref[1])
        grow_b = _dot(packed_b, e_ref[2])
        xscale_b = _dot(packed_b, e_ref[3])
        hdecay_f = grow_f[CHUNK - 1:CHUNK, :]
        hdecay_b = grow_b[0:1, :]

        for g in range(SSD_GROUPS):
            gs = slice(g * GROUP_WIDTH, (g + 1) * GROUP_WIDTH)
            bs = slice(b_off + g * SSD_STATE, b_off + (g + 1) * SSD_STATE)
            cs = slice(c_off + g * SSD_STATE, c_off + (g + 1) * SSD_STATE)

            xs = xf_ref[0, tf, gs]
            bm = xf_ref[0, tf, bs]
            cm = xf_ref[0, tf, cs]
            scores = _dot_nt(cm, bm)
            w_heads = []
            x_blocks = []
            for r in range(HEADS_PER_GROUP):
                hd = g * HEADS_PER_GROUP + r
                expo = jnp.where(below, cols[:, hd:hd + 1] - row_f[hd:hd + 1, :],
                                 jnp.where(above, row_b[hd:hd + 1, :] - cols[:, nh + hd:nh + hd + 1],
                                           row_d[hd:hd + 1, :]))
                w_heads.append((scores * jnp.exp(expo)).astype(BF16))
                x_blocks.append(jnp.where(head_lane == r, xs, jnp.zeros_like(xs)))
            xs32 = xs.astype(F32)
            y = _dot(jnp.concatenate(w_heads, axis=1), jnp.concatenate(x_blocks, axis=0))
            y = y + xs32 * dexp_ref[:, gs] + _dot(cm, hf_ref[g].astype(BF16)) * grow_f[:, gs]
            yf_ref[0, tf, gs] = y.astype(yf_ref.dtype)
            xt = (xs32 * xscale_f[:, gs]).astype(BF16)
            hf_ref[g] = hf_ref[g] * hdecay_f[:, gs] + _dot_tn(bm, xt)

            xs = xb_ref[0, tb, gs]
            bm = xb_ref[0, tb, bs]
            cm = xb_ref[0, tb, cs]
            yb_ref[0, tb, gs] = (_dot(cm, hb_ref[g].astype(BF16)) * grow_b[:, gs]).astype(yb_ref.dtype)
            xt = (xs.astype(F32) * xscale_b[:, gs]).astype(BF16)
            hb_ref[g] = hb_ref[g] * hdecay_b[:, gs] + _dot_tn(bm, xt)


def _ssd(xbc, rows, cols, packed, d_expand, expanders):
    bsz, length, width = xbc.shape
    span = SSD_STEP_CHUNKS * CHUNK
    n_steps = length // span
    d_ssd = SSD_HEADS * SSD_HEADDIM
    fwd = lambda b, i: (b, i, 0)
    bwd = lambda b, i: (b, n_steps - 1 - i, 0)
    in_specs = [
        pl.BlockSpec((1, span, width), fwd),
        pl.BlockSpec((1, span, width), bwd),
        pl.BlockSpec((1, SSD_STEP_CHUNKS, 3 * SSD_HEADS, CHUNK), lambda b, i: (b, i, 0, 0)),
        pl.BlockSpec((1, span, LANES), fwd),
        pl.BlockSpec((1, span, LANES), fwd),
        pl.BlockSpec((1, span, LANES), bwd),
        _const_spec(d_expand.shape), _const_spec(expanders.shape),
    ]
    out_shape = (jax.ShapeDtypeStruct((bsz, length, d_ssd), BF16),
                 jax.ShapeDtypeStruct((bsz, length, d_ssd), BF16))
    out_specs = (pl.BlockSpec((1, span, d_ssd), fwd), pl.BlockSpec((1, span, d_ssd), bwd))
    state = pltpu.VMEM((SSD_GROUPS, SSD_STATE, GROUP_WIDTH), F32)
    return pl.pallas_call(
        _ssd_kernel, grid=(bsz, n_steps), in_specs=in_specs, out_specs=out_specs,
        out_shape=out_shape, scratch_shapes=[state, state],
        compiler_params=pltpu.CompilerParams(
            dimension_semantics=("parallel", "arbitrary"), vmem_limit_bytes=VMEM_LIMIT_BYTES),
        name="ssd",
    )(xbc, xbc, rows, cols, packed, packed, d_expand, expanders)


def _mixout_kernel(h_ref, yf_ref, yb_ref, z_ref, ysc_ref, gw_ref, wo_ref, out_ref):
    d_ssd = yf_ref.shape[-1]
    y = (yf_ref[0].astype(F32) + yb_ref[0].astype(F32)) * _silu(z_ref[0].astype(F32))
    normed = []
    for g in range(SSD_GROUPS):
        gs = slice(g * GROUP_WIDTH, (g + 1) * GROUP_WIDTH)
        normed.append(_rms(y[:, gs], gw_ref[:, gs]).astype(BF16))
    yn = jnp.concatenate(normed, axis=1)
    out = _dot(yn, wo_ref[0:d_ssd, :]) + _dot(ysc_ref[0], wo_ref[d_ssd:, :])
    out_ref[0] = h_ref[0] + out


def _mixout(h, y_f, y_b, z, y_sc, gnorm_w, w_out, *, tile):
    bsz, length, d = h.shape
    row_spec = lambda width: pl.BlockSpec((1, tile, width), lambda b, t: (b, t, 0))
    in_specs = [row_spec(d), row_spec(y_f.shape[-1]), row_spec(y_b.shape[-1]),
                row_spec(z.shape[-1]), row_spec(y_sc.shape[-1]),
                _const_spec(gnorm_w.shape), _const_spec(w_out.shape)]
    return pl.pallas_call(
        _mixout_kernel, grid=(bsz, length // tile), in_specs=in_specs, out_specs=row_spec(d),
        out_shape=jax.ShapeDtypeStruct(h.shape, F32),
        compiler_params=pltpu.CompilerParams(
            dimension_semantics=("parallel", "parallel"), vmem_limit_bytes=VMEM_LIMIT_BYTES),
        name="mixout",
    )(h, y_f, y_b, z, y_sc, gnorm_w, w_out)


FF_COL_BLOCK = 256


def _strand_conv3(p_ref, w_ref, rows):
    sub = lax.broadcasted_iota(jnp.int32, (F32_SUBLANES, p_ref.shape[1]), 0)
    width = p_ref.shape[1]
    before = jnp.broadcast_to(p_ref[rows:rows + 1, :], (F32_SUBLANES, width))
    after = jnp.broadcast_to(p_ref[rows + 1:rows + 2, :], (F32_SUBLANES, width))
    head = jnp.where(sub == 0, before, pltpu.roll(p_ref[rows - F32_SUBLANES:rows, :], 1, 0))
    tail = jnp.where(sub == F32_SUBLANES - 1, after,
                     pltpu.roll(p_ref[0:F32_SUBLANES, :], F32_SUBLANES - 1, 0))
    prev = jnp.concatenate([head, p_ref[0:rows - F32_SUBLANES, :]], axis=0)
    nxt = jnp.concatenate([p_ref[F32_SUBLANES:rows, :], tail], axis=0)
    return prev * w_ref[0:1, :] + p_ref[0:rows, :] * w_ref[1:2, :] + nxt * w_ref[2:3, :]


def _strand_blocks(strand, d):
    return [(s, w, j) for s in range(F32_SUBLANES) for w in range(strand // F32_SUBLANES)
            for j in range(d // LANES)]


def _ffn_kernel(hp_ref, h_ref, hn_ref, nw_ref, wup_ref, cw_ref, cb_ref, wdn_ref, fw_ref,
                out_ref, hs_ref, lhs_ref, act_ref, p_ref, *, final_norm):
    rows = h_ref.shape[1]
    d = h_ref.shape[2]
    strand = rows // F32_SUBLANES
    d_ff = wdn_ref.shape[0]
    t = pl.program_id(1)
    last = pl.num_programs(1) - 1
    norm_w = nw_ref[...]

    for s, w, j in _strand_blocks(strand, d):
        tok = strand * s + F32_SUBLANES * w
        hs_ref[j, pl.ds(strand * w + s, F32_SUBLANES, stride=F32_SUBLANES), :] = (
            h_ref[0, tok:tok + F32_SUBLANES, j * LANES:(j + 1) * LANES])
    h_strand = jnp.concatenate([hs_ref[j] for j in range(d // LANES)], axis=1)
    before = _rms(hp_ref[0], norm_w)[F32_SUBLANES - 1:F32_SUBLANES] * (t > 0).astype(F32)
    after = _rms(hn_ref[0], norm_w)[0:1] * (t < last).astype(F32)
    halo_row = lax.broadcasted_iota(jnp.int32, (HALO, d), 0)
    halo = jnp.where(halo_row == 0, jnp.broadcast_to(before, (HALO, d)),
                     jnp.where(halo_row == 1, jnp.broadcast_to(after, (HALO, d)), 0.0))
    lhs_ref[...] = jnp.concatenate([_rms(h_strand, norm_w), halo], axis=0).astype(BF16)
    full = lhs_ref[...]

    slot = 0
    for c in range(0, d_ff, FF_COL_BLOCK):
        gs = slice(c, c + FF_COL_BLOCK)
        us = slice(d_ff + c, d_ff + c + FF_COL_BLOCK)
        gbuf = p_ref.at[slot, 0]
        ubuf = p_ref.at[slot, 1]
        slot = 1 - slot
        gbuf[...] = _dot(full, wup_ref[:, gs])
        ubuf[...] = _dot(full, wup_ref[:, us])
        gate = _strand_conv3(gbuf, cw_ref.at[:, gs], rows) + cb_ref[:, gs]
        up = _strand_conv3(ubuf, cw_ref.at[:, us], rows) + cb_ref[:, us]
        act_ref[:, gs] = (_silu(gate) * up).astype(BF16)
    out = h_strand + _dot(act_ref[...], wdn_ref[...])
    if final_norm:
        out = _rms(out, fw_ref[...])
    for j in range(d // LANES):
        hs_ref[j] = out[:, j * LANES:(j + 1) * LANES]
    for s, w, j in _strand_blocks(strand, d):
        tok = strand * s + F32_SUBLANES * w
        out_ref[0, tok:tok + F32_SUBLANES, j * LANES:(j + 1) * LANES] = (
            hs_ref[j, pl.ds(strand * w + s, F32_SUBLANES, stride=F32_SUBLANES), :])


def _ffn(h, norm_w, w_up, conv_w, conv_b, w_down, final_w, *, tile, final_norm):
    bsz, length, d = h.shape
    d_ff = w_down.shape[0]
    blk8 = tile // F32_SUBLANES
    last8 = length // F32_SUBLANES - 1
    row_spec = pl.BlockSpec((1, tile, d), lambda b, t: (b, t, 0))
    in_specs = [
        pl.BlockSpec((1, F32_SUBLANES, d), lambda b, t: (b, jnp.maximum(t * blk8 - 1, 0), 0)),
        row_spec,
        pl.BlockSpec((1, F32_SUBLANES, d), lambda b, t: (b, jnp.minimum((t + 1) * blk8, last8), 0)),
        _const_spec(norm_w.shape), _const_spec(w_up.shape), _const_spec(conv_w.shape),
        _const_spec(conv_b.shape), _const_spec(w_down.shape), _const_spec(final_w.shape),
    ]
    return pl.pallas_call(
        functools.partial(_ffn_kernel, final_norm=final_norm),
        grid=(bsz, length // tile), in_specs=in_specs, out_specs=row_spec,
        out_shape=jax.ShapeDtypeStruct(h.shape, F32),
        scratch_shapes=[pltpu.VMEM((d // LANES, tile, LANES), F32), pltpu.VMEM((tile + HALO, d), BF16),
                        pltpu.VMEM((tile, d_ff), BF16),
                        pltpu.VMEM((2, 2, tile + HALO, FF_COL_BLOCK), F32)],
        compiler_params=pltpu.CompilerParams(
            dimension_semantics=("parallel", "parallel"), vmem_limit_bytes=VMEM_LIMIT_BYTES),
        name="ffn",
    )(h, h, h, norm_w, w_up, conv_w, conv_b, w_down, final_w)


def _lane_rows(fwd, bwd):
    v = jnp.concatenate([fwd, bwd]).astype(F32)
    return jnp.broadcast_to(v[:, None], (v.shape[0], LANES))


def _head_expanders():
    j = jnp.arange(LANES)[None, :, None]
    k = jnp.arange(N_EXPAND)[:, None, None]
    head = jnp.arange(SSD_HEADS * SSD_HEADDIM)[None, None, :] // SSD_HEADDIM
    half = N_EXPAND * SSD_HEADS
    return ((j == k * SSD_HEADS + head) | (j == half + k * SSD_HEADS + head)).astype(BF16)


def kernel(x, w_in, conv_xbc_w, conv_xbc_b, dt_bias_f, dt_bias_b, a_log_f, a_log_b, d_skip,
           ssd_norm_w, sc_conv_w, w_out, norm1_w, norm2_w, w_ffn_up, ffn_conv_w, ffn_conv_b,
           w_ffn_down, final_norm_w):
    depth = w_in.shape[0]
    d_ssd = SSD_HEADS * SSD_HEADDIM
    conv_xbc = d_ssd + 2 * SSD_GROUPS * SSD_STATE
    d_sc = sc_conv_w.shape[-1]
    o_xbc = d_ssd
    o_dt = o_xbc + conv_xbc
    o_scb = o_dt + 2 * SSD_HEADS
    o_scc = o_scb + d_sc
    o_scv = o_scc + d_sc
    length = x.shape[1]
    tile = min(512, length)
    expanders = _head_expanders()
    row = lambda v: v.astype(F32).reshape(1, -1)

    h = x
    for l in range(depth):
        w = w_in[l]
        wz = w[:, :o_xbc].astype(BF16)
        wxbc = w[:, o_xbc:o_dt].astype(BF16)
        wdt = jnp.pad(w[:, o_dt:o_scb], ((0, 0), (0, LANES - 2 * SSD_HEADS))).astype(BF16)
        wscb = w[:, o_scb:o_scc].astype(BF16)
        wscc = w[:, o_scc:o_scv].astype(BF16)
        wscv = w[:, o_scv:].astype(BF16)
        z, xbc, y_sc, dt_raw = _inproj(
            h, row(norm1_w[l]), wz, wxbc, wscb, wscc, wscv, wdt,
            conv_xbc_w[l].astype(F32), row(conv_xbc_b[l]), sc_conv_w[l].astype(F32), tile=tile)
        rows, cols, packed = _ssd_prep(
            dt_raw, _lane_rows(dt_bias_f[l], dt_bias_b[l]), _lane_rows(a_log_f[l], a_log_b[l]))
        y_f, y_b = _ssd(xbc, rows, cols, packed,
                        jnp.repeat(d_skip[l].astype(F32), SSD_HEADDIM).reshape(1, -1), expanders)
        h = _mixout(h, y_f, y_b, z, y_sc, row(ssd_norm_w[l]), w_out[l].astype(BF16), tile=tile)
        h = _ffn(h, row(norm2_w[l]), w_ffn_up[l].astype(BF16), ffn_conv_w[l].astype(F32),
                 row(ffn_conv_b[l]), w_ffn_down[l].astype(BF16), row(final_norm_w),
                 tile=tile, final_norm=(l == depth - 1))
    return h
```

```python
import functools

import jax
import jax.numpy as jnp
from jax import lax
from jax.experimental import pallas as pl
from jax.experimental.pallas import tpu as pltpu

F32 = jnp.float32
BF16 = jnp.bfloat16

RMS_EPS = 1e-5
SSD_HEADDIM = 64
SSD_GROUPS = 4
SSD_STATE = 128
CHUNK = 128
HEADS_PER_GROUP = 4
SSD_HEADS = SSD_GROUPS * HEADS_PER_GROUP
GROUP_WIDTH = HEADS_PER_GROUP * SSD_HEADDIM

LANES = 128
F32_SUBLANES = 8
BF16_SUBLANES = 16
HALO = BF16_SUBLANES
VMEM_LIMIT_BYTES = 56 * 1024 * 1024


def _dot(a, b):
    return jnp.dot(a, b, preferred_element_type=F32)


def _dot_nt(a, b):
    return lax.dot_general(a, b, (((1,), (1,)), ((), ())), preferred_element_type=F32)


def _dot_tn(a, b):
    return lax.dot_general(a, b, (((0,), (0,)), ((), ())), preferred_element_type=F32)


def _silu(x):
    return x * (1.0 / (1.0 + jnp.exp(-x)))


def _rms(v, w):
    ms = jnp.mean(v * v, axis=-1, keepdims=True)
    return v * lax.rsqrt(ms + RMS_EPS) * w


def _const_spec(shape):
    zeros = (0,) * len(shape)
    return pl.BlockSpec(shape, lambda *_: zeros, pipeline_mode=pl.Buffered(1))


def _layer_spec(stacked, layer, cols=None, col_block=0):
    _, rows, width = stacked.shape
    return pl.BlockSpec((None, rows, width if cols is None else cols),
                        lambda *_: (layer, 0, col_block), pipeline_mode=pl.Buffered(1))


def _halo_lhs(prev_ref, main_ref, next_ref, norm_w, lhs_ref):
    t = pl.program_id(1)
    last = pl.num_programs(1) - 1
    d = main_ref.shape[-1]
    prev = _rms(prev_ref[0], norm_w) * (t > 0).astype(F32)
    nxt = _rms(next_ref[0], norm_w) * (t < last).astype(F32)
    main = _rms(main_ref[0], norm_w)
    zeros = jnp.zeros((HALO - F32_SUBLANES, d), F32)
    lhs_ref[...] = jnp.concatenate([zeros, prev, main, nxt, zeros], axis=0).astype(BF16)


def _conv3(p_ref, w_ref, rows):
    return (p_ref[HALO - 1:HALO - 1 + rows, :] * w_ref[0:1, :]
            + p_ref[HALO:HALO + rows, :] * w_ref[1:2, :]
            + p_ref[HALO + 1:HALO + 1 + rows, :] * w_ref[2:3, :])


IN_COL_BLOCK = 512
INPROJ_TILE = 1024


def _inproj_kernel(hp_ref, h_ref, hn_ref, nw_ref, wz_ref, wxa_ref, wxb_ref, wscb_ref, wscc_ref,
                   wscv_ref, wdt_ref, cw_ref, cb_ref, scw_ref, bias_ref, alog_ref,
                   z_out, xbc_out, ysc_out, rows_out, cols_out, packed_out, lhs_ref, p_ref):
    rows = h_ref.shape[1]
    _halo_lhs(hp_ref, h_ref, hn_ref, nw_ref[...], lhs_ref)
    centre = lhs_ref[HALO:HALO + rows, :]
    full = lhs_ref[...]

    _ssd_scalars(_dot(centre, wdt_ref[...]), bias_ref, alog_ref, rows_out, cols_out, packed_out)
    for c in range(0, wz_ref.shape[1], IN_COL_BLOCK):
        cs = slice(c, c + IN_COL_BLOCK)
        z_out[0, :, cs] = _dot(centre, wz_ref[:, cs]).astype(z_out.dtype)
    slot = 0
    half = wxa_ref.shape[1]
    for c in range(0, 2 * half, IN_COL_BLOCK):
        cs = slice(c, c + IN_COL_BLOCK)
        w_ref = wxa_ref if c < half else wxb_ref
        pbuf = p_ref.at[slot]
        slot = 1 - slot
        pbuf[...] = _dot(full, w_ref[:, c % half:c % half + IN_COL_BLOCK])
        conv = _conv3(pbuf, cw_ref.at[:, cs], rows) + cb_ref[:, cs]
        xbc_out[0, :, cs] = _silu(conv).astype(xbc_out.dtype)
    for c in range(0, wscb_ref.shape[1], IN_COL_BLOCK):
        cs = slice(c, c + IN_COL_BLOCK)
        pbuf = p_ref.at[slot]
        slot = 1 - slot
        pbuf[...] = _dot(full, wscc_ref[:, cs]) * _dot(full, wscv_ref[:, cs])
        gate = _dot(centre, wscb_ref[:, cs])
        ysc_out[0, :, cs] = (gate * _conv3(pbuf, scw_ref.at[:, cs], rows)).astype(ysc_out.dtype)


def _inproj(h, layer, norm_w, w_in, conv_w, conv_b, sc_w, dt_bias, a_log, *, tile):
    bsz, length, d = h.shape
    n_tiles = length // tile
    blk8 = tile // F32_SUBLANES
    last8 = length // F32_SUBLANES - 1
    grid = (bsz, n_tiles)
    row_spec = lambda width: pl.BlockSpec((1, tile, width), lambda b, t: (b, t, 0))
    w_block = lambda j: _layer_spec(w_in, layer, cols=d, col_block=j)
    in_specs = [
        pl.BlockSpec((1, F32_SUBLANES, d), lambda b, t: (b, jnp.maximum(t * blk8 - 1, 0), 0)),
        row_spec(d),
        pl.BlockSpec((1, F32_SUBLANES, d), lambda b, t: (b, jnp.minimum((t + 1) * blk8, last8), 0)),
        _layer_spec(norm_w, layer),
        w_block(0), w_block(1), w_block(2), w_block(3), w_block(4), w_block(5),
        _layer_spec(w_in, layer, cols=LANES, col_block=6 * d // LANES),
        _layer_spec(conv_w, layer), _layer_spec(conv_b, layer), _layer_spec(sc_w, layer),
        _layer_spec(dt_bias, layer), _layer_spec(a_log, layer),
    ]
    out_shape = (
        jax.ShapeDtypeStruct((bsz, length, d), BF16),
        jax.ShapeDtypeStruct((bsz, length, 2 * d), BF16),
        jax.ShapeDtypeStruct((bsz, length, d), BF16),
        jax.ShapeDtypeStruct((bsz, length // CHUNK, 3 * SSD_HEADS, CHUNK), F32),
        jax.ShapeDtypeStruct((bsz, length, LANES), F32),
        jax.ShapeDtypeStruct((bsz, length, LANES), BF16),
    )
    out_specs = (row_spec(d), row_spec(2 * d), row_spec(d),
                 pl.BlockSpec((1, tile // CHUNK, 3 * SSD_HEADS, CHUNK), lambda b, t: (b, t, 0, 0)),
                 row_spec(LANES), row_spec(LANES))
    return pl.pallas_call(
        _inproj_kernel, grid=grid, in_specs=in_specs, out_specs=out_specs, out_shape=out_shape,
        scratch_shapes=[pltpu.VMEM((tile + 2 * HALO, d), BF16),
                        pltpu.VMEM((2, tile + 2 * HALO, IN_COL_BLOCK), F32)],
        compiler_params=pltpu.CompilerParams(
            dimension_semantics=("parallel", "parallel"), vmem_limit_bytes=VMEM_LIMIT_BYTES),
        name="inproj",
    )(h, h, h, norm_w, w_in, w_in, w_in, w_in, w_in, w_in, w_in, conv_w, conv_b, sc_w, dt_bias, a_log)


N_EXPAND = 4


def _softplus(x):
    return jnp.maximum(x, 0.0) + jnp.log(1.0 + jnp.exp(-jnp.abs(x)))


def _cumsum_lanes(a):
    lane = lax.broadcasted_iota(jnp.int32, a.shape, 1)
    shift = 1
    while shift < a.shape[1]:
        a = a + jnp.where(lane >= shift, pltpu.roll(a, shift, 1), 0.0)
        shift *= 2
    return a


def _pad_rows(v, rows):
    return jnp.concatenate([v, jnp.zeros((rows - v.shape[0], v.shape[1]), v.dtype)], axis=0)


def _ssd_scalars(dt_raw, bias_ref, alog_ref, rows_out, cols_out, packed_out):
    nh = SSD_HEADS
    n_chunks = dt_raw.shape[0] // CHUNK
    raw = jnp.concatenate(
        [dt_raw[c * CHUNK:(c + 1) * CHUNK, :].T[0:2 * nh] for c in range(n_chunks)], axis=0)
    bias = jnp.concatenate([bias_ref[...]] * n_chunks, axis=0)
    neg_a = -jnp.exp(jnp.concatenate([alog_ref[...]] * n_chunks, axis=0))
    dt = _softplus(raw + bias)
    a = dt * neg_a
    acs = _cumsum_lanes(a)
    ecs = acs - a
    tot = jnp.broadcast_to(acs[:, CHUNK - 1:CHUNK], acs.shape)
    log_dt = jnp.log(dt)
    grow_f = jnp.exp(acs)
    xscale_f = dt * jnp.exp(tot - acs)
    grow_b = jnp.exp(tot - ecs)
    xscale_b = dt * jnp.exp(ecs)
    for c in range(n_chunks):
        fwd = slice(2 * nh * c, 2 * nh * c + nh)
        bwd = slice(2 * nh * c + nh, 2 * nh * (c + 1))
        rows_out[0, c] = jnp.concatenate(
            [acs[fwd] - log_dt[fwd],
             ecs[bwd] + log_dt[bwd],
             jnp.log(dt[fwd] + dt[bwd])], axis=0)
        tok = slice(c * CHUNK, (c + 1) * CHUNK)
        cols_out[0, tok, :] = _pad_rows(jnp.concatenate([acs[fwd], ecs[bwd]], axis=0), CHUNK).T
        wide = jnp.concatenate([grow_f[fwd], xscale_f[fwd], grow_b[bwd], xscale_b[bwd]], axis=0)
        wide_hi = wide.astype(BF16).astype(F32)
        packed_out[0, tok, :] = jnp.concatenate([wide_hi, wide - wide_hi], axis=0).T.astype(BF16)


SSD_STEP_CHUNKS = 4


def _ssd_kernel(xf_ref, xb_ref, rows_ref, cols_ref, pf_ref, pb_ref, dexp_ref, e_ref,
                yf_ref, yb_ref, *state_refs):
    hf_ref = state_refs[:SSD_GROUPS]
    hb_ref = state_refs[SSD_GROUPS:]
    d_ssd = SSD_HEADS * SSD_HEADDIM
    b_off = d_ssd
    c_off = d_ssd + SSD_GROUPS * SSD_STATE
    nh = SSD_HEADS

    @pl.when(pl.program_id(1) == 0)
    def _():
        for ref in state_refs:
            ref[...] = jnp.zeros_like(ref)

    row_i = lax.broadcasted_iota(jnp.int32, (CHUNK, CHUNK), 0)
    col_i = lax.broadcasted_iota(jnp.int32, (CHUNK, CHUNK), 1)
    below = row_i > col_i
    above = row_i < col_i
    head_lane = lax.broadcasted_iota(jnp.int32, (CHUNK, GROUP_WIDTH), 1) // SSD_HEADDIM

    for k in range(SSD_STEP_CHUNKS):
        tf = slice(k * CHUNK, (k + 1) * CHUNK)
        tb = slice((SSD_STEP_CHUNKS - 1 - k) * CHUNK, (SSD_STEP_CHUNKS - k) * CHUNK)
        row_f = rows_ref[0, k, 0:nh, :]
        row_b = rows_ref[0, k, nh:2 * nh, :]
        row_d = rows_ref[0, k, 2 * nh:3 * nh, :]
        cols = cols_ref[0, tf, :]
        packed_f = pf_ref[0, tf, :]
        packed_b = pb_ref[0, tb, :]

        for g in range(SSD_GROUPS):
            gs = slice(g * GROUP_WIDTH, (g + 1) * GROUP_WIDTH)
            bs = slice(b_off + g * SSD_STATE, b_off + (g + 1) * SSD_STATE)
            cs = slice(c_off + g * SSD_STATE, c_off + (g + 1) * SSD_STATE)
            grow_f = _dot(packed_f, e_ref[0, :, gs])
            xscale_f = _dot(packed_f, e_ref[1, :, gs])
            grow_b = _dot(packed_b, e_ref[2, :, gs])
            xscale_b = _dot(packed_b, e_ref[3, :, gs])
            hdecay_f = grow_f[CHUNK - 1:CHUNK, :]
            hdecay_b = grow_b[0:1, :]

            xs = xf_ref[0, tf, gs]
            bm = xf_ref[0, tf, bs]
            cm = xf_ref[0, tf, cs]
            scores = _dot_nt(cm, bm)
            w_heads = []
            x_blocks = []
            for r in range(HEADS_PER_GROUP):
                hd = g * HEADS_PER_GROUP + r
                expo = jnp.where(below, cols[:, hd:hd + 1] - row_f[hd:hd + 1, :],
                                 jnp.where(above, row_b[hd:hd + 1, :] - cols[:, nh + hd:nh + hd + 1],
                                           row_d[hd:hd + 1, :]))
                w_heads.append((scores * jnp.exp(expo)).astype(BF16))
                x_blocks.append(jnp.where(head_lane == r, xs, jnp.zeros_like(xs)))
            xs32 = xs.astype(F32)
            y = _dot(jnp.concatenate(w_heads, axis=1), jnp.concatenate(x_blocks, axis=0))
            y = y + xs32 * dexp_ref[:, gs] + _dot(cm, hf_ref[g][...].astype(BF16)) * grow_f
            yf_ref[0, tf, gs] = y.astype(yf_ref.dtype)
            xt = (xs32 * xscale_f).astype(BF16)
            hf_ref[g][...] = hf_ref[g][...] * hdecay_f + _dot_tn(bm, xt)

            xs = xb_ref[0, tb, gs]
            bm = xb_ref[0, tb, bs]
            cm = xb_ref[0, tb, cs]
            yb_ref[0, tb, gs] = (_dot(cm, hb_ref[g][...].astype(BF16)) * grow_b).astype(yb_ref.dtype)
            xt = (xs.astype(F32) * xscale_b).astype(BF16)
            hb_ref[g][...] = hb_ref[g][...] * hdecay_b + _dot_tn(bm, xt)


def _ssd(xbc, rows, cols, packed, layer, d_expand, expanders):
    bsz, length, width = xbc.shape
    span = SSD_STEP_CHUNKS * CHUNK
    n_steps = length // span
    d_ssd = SSD_HEADS * SSD_HEADDIM
    fwd = lambda b, i: (b, i, 0)
    bwd = lambda b, i: (b, n_steps - 1 - i, 0)
    in_specs = [
        pl.BlockSpec((1, span, width), fwd),
        pl.BlockSpec((1, span, width), bwd),
        pl.BlockSpec((1, SSD_STEP_CHUNKS, 3 * SSD_HEADS, CHUNK), lambda b, i: (b, i, 0, 0)),
        pl.BlockSpec((1, span, LANES), fwd),
        pl.BlockSpec((1, span, LANES), fwd),
        pl.BlockSpec((1, span, LANES), bwd),
        _layer_spec(d_expand, layer), _const_spec(expanders.shape),
    ]
    out_shape = (jax.ShapeDtypeStruct((bsz, length, d_ssd), BF16),
                 jax.ShapeDtypeStruct((bsz, length, d_ssd), BF16))
    out_specs = (pl.BlockSpec((1, span, d_ssd), fwd), pl.BlockSpec((1, span, d_ssd), bwd))
    state = pltpu.VMEM((SSD_STATE, GROUP_WIDTH), F32)
    return pl.pallas_call(
        _ssd_kernel, grid=(bsz, n_steps), in_specs=in_specs, out_specs=out_specs,
        out_shape=out_shape, scratch_shapes=[state] * (2 * SSD_GROUPS),
        compiler_params=pltpu.CompilerParams(
            dimension_semantics=("parallel", "arbitrary"), vmem_limit_bytes=VMEM_LIMIT_BYTES),
        name="ssd",
    )(xbc, xbc, rows, cols, packed, packed, d_expand, expanders)


MIXOUT_TILE = 1024
MIXOUT_ROW_BLOCK = 256


def _mixout_kernel(h_ref, yf_ref, yb_ref, z_ref, ysc_ref, gw_ref, wo_ref, out_ref):
    d_ssd = yf_ref.shape[-1]
    rows = h_ref.shape[1]
    for r in range(0, rows, MIXOUT_ROW_BLOCK):
        rs = slice(r, min(r + MIXOUT_ROW_BLOCK, rows))
        y = (yf_ref[0, rs, :].astype(F32) + yb_ref[0, rs, :].astype(F32)) * _silu(z_ref[0, rs, :].astype(F32))
        normed = []
        for g in range(SSD_GROUPS):
            gs = slice(g * GROUP_WIDTH, (g + 1) * GROUP_WIDTH)
            normed.append(_rms(y[:, gs], gw_ref[:, gs]).astype(BF16))
        yn = jnp.concatenate(normed, axis=1)
        out = _dot(yn, wo_ref[0:d_ssd, :]) + _dot(ysc_ref[0, rs, :], wo_ref[d_ssd:, :])
        out_ref[0, rs, :] = h_ref[0, rs, :] + out


def _mixout(h, y_f, y_b, z, y_sc, layer, gnorm_w, w_out, *, tile):
    bsz, length, d = h.shape
    row_spec = lambda width: pl.BlockSpec((1, tile, width), lambda b, t: (b, t, 0))
    in_specs = [row_spec(d), row_spec(y_f.shape[-1]), row_spec(y_b.shape[-1]),
                row_spec(z.shape[-1]), row_spec(y_sc.shape[-1]),
                _layer_spec(gnorm_w, layer), _layer_spec(w_out, layer)]
    return pl.pallas_call(
        _mixout_kernel, grid=(bsz, length // tile), in_specs=in_specs, out_specs=row_spec(d),
        out_shape=jax.ShapeDtypeStruct(h.shape, F32),
        compiler_params=pltpu.CompilerParams(
            dimension_semantics=("parallel", "parallel"), vmem_limit_bytes=VMEM_LIMIT_BYTES),
        name="mixout",
    )(h, y_f, y_b, z, y_sc, gnorm_w, w_out)


FF_COL_BLOCK = 256
FFN_TILE = 512


def _strand_conv3(p_ref, w_ref, rows):
    sub = lax.broadcasted_iota(jnp.int32, (F32_SUBLANES, p_ref.shape[1]), 0)
    width = p_ref.shape[1]
    before = jnp.broadcast_to(p_ref[rows:rows + 1, :], (F32_SUBLANES, width))
    after = jnp.broadcast_to(p_ref[rows + 1:rows + 2, :], (F32_SUBLANES, width))
    head = jnp.where(sub == 0, before, pltpu.roll(p_ref[rows - F32_SUBLANES:rows, :], 1, 0))
    tail = jnp.where(sub == F32_SUBLANES - 1, after,
                     pltpu.roll(p_ref[0:F32_SUBLANES, :], F32_SUBLANES - 1, 0))
    prev = jnp.concatenate([head, p_ref[0:rows - F32_SUBLANES, :]], axis=0)
    nxt = jnp.concatenate([p_ref[F32_SUBLANES:rows, :], tail], axis=0)
    return prev * w_ref[0:1, :] + p_ref[0:rows, :] * w_ref[1:2, :] + nxt * w_ref[2:3, :]


def _strand_blocks(strand, d):
    return [(s, w, j) for s in range(F32_SUBLANES) for w in range(strand // F32_SUBLANES)
            for j in range(d // LANES)]


def _ffn_kernel(hp_ref, h_ref, hn_ref, nw_ref, wup_ref, cw_ref, cb_ref, wdn_ref, fw_ref,
                out_ref, hs_ref, lhs_ref, act_ref, p_ref, *, final_norm):
    rows = h_ref.shape[1]
    d = h_ref.shape[2]
    strand = rows // F32_SUBLANES
    d_ff = wdn_ref.shape[0]
    t = pl.program_id(1)
    last = pl.num_programs(1) - 1
    norm_w = nw_ref[...]

    for s, w, j in _strand_blocks(strand, d):
        tok = strand * s + F32_SUBLANES * w
        hs_ref[j, pl.ds(strand * w + s, F32_SUBLANES, stride=F32_SUBLANES), :] = (
            h_ref[0, tok:tok + F32_SUBLANES, j * LANES:(j + 1) * LANES])
    h_strand = jnp.concatenate([hs_ref[j] for j in range(d // LANES)], axis=1)
    before = _rms(hp_ref[0], norm_w)[F32_SUBLANES - 1:F32_SUBLANES] * (t > 0).astype(F32)
    after = _rms(hn_ref[0], norm_w)[0:1] * (t < last).astype(F32)
    halo_row = lax.broadcasted_iota(jnp.int32, (HALO, d), 0)
    halo = jnp.where(halo_row == 0, jnp.broadcast_to(before, (HALO, d)),
                     jnp.where(halo_row == 1, jnp.broadcast_to(after, (HALO, d)), 0.0))
    lhs_ref[...] = jnp.concatenate([_rms(h_strand, norm_w), halo], axis=0).astype(BF16)
    full = lhs_ref[...]

    slot = 0
    for c in range(0, d_ff, FF_COL_BLOCK):
        gs = slice(c, c + FF_COL_BLOCK)
        us = slice(d_ff + c, d_ff + c + FF_COL_BLOCK)
        gbuf = p_ref.at[slot, 0]
        ubuf = p_ref.at[slot, 1]
        slot = 1 - slot
        gbuf[...] = _dot(full, wup_ref[:, gs])
        ubuf[...] = _dot(full, wup_ref[:, us])
        gate = _strand_conv3(gbuf, cw_ref.at[:, gs], rows) + cb_ref[:, gs]
        up = _strand_conv3(ubuf, cw_ref.at[:, us], rows) + cb_ref[:, us]
        act_ref[:, gs] = (_silu(gate) * up).astype(BF16)
    out = h_strand + _dot(act_ref[...], wdn_ref[...])
    if final_norm:
        out = _rms(out, fw_ref[...])
    for j in range(d // LANES):
        hs_ref[j] = out[:, j * LANES:(j + 1) * LANES]
    for s, w, j in _strand_blocks(strand, d):
        tok = strand * s + F32_SUBLANES * w
        out_ref[0, tok:tok + F32_SUBLANES, j * LANES:(j + 1) * LANES] = (
            hs_ref[j, pl.ds(strand * w + s, F32_SUBLANES, stride=F32_SUBLANES), :])


def _ffn(h, layer, norm_w, w_up, conv_w, conv_b, w_down, final_w, *, tile, final_norm):
    bsz, length, d = h.shape
    d_ff = w_down.shape[1]
    blk8 = tile // F32_SUBLANES
    last8 = length // F32_SUBLANES - 1
    row_spec = pl.BlockSpec((1, tile, d), lambda b, t: (b, t, 0))
    in_specs = [
        pl.BlockSpec((1, F32_SUBLANES, d), lambda b, t: (b, jnp.maximum(t * blk8 - 1, 0), 0)),
        row_spec,
        pl.BlockSpec((1, F32_SUBLANES, d), lambda b, t: (b, jnp.minimum((t + 1) * blk8, last8), 0)),
        _layer_spec(norm_w, layer), _layer_spec(w_up, layer), _layer_spec(conv_w, layer),
        _layer_spec(conv_b, layer), _layer_spec(w_down, layer), _const_spec(final_w.shape),
    ]
    return pl.pallas_call(
        functools.partial(_ffn_kernel, final_norm=final_norm),
        grid=(bsz, length // tile), in_specs=in_specs, out_specs=row_spec,
        out_shape=jax.ShapeDtypeStruct(h.shape, F32),
        scratch_shapes=[pltpu.VMEM((d // LANES, tile, LANES), F32), pltpu.VMEM((tile + HALO, d), BF16),
                        pltpu.VMEM((tile, d_ff), BF16),
                        pltpu.VMEM((2, 2, tile + HALO, FF_COL_BLOCK), F32)],
        compiler_params=pltpu.CompilerParams(
            dimension_semantics=("parallel", "parallel"), vmem_limit_bytes=VMEM_LIMIT_BYTES),
        name="ffn",
    )(h, h, h, norm_w, w_up, conv_w, conv_b, w_down, final_w)


def _lane_rows(fwd, bwd):
    v = jnp.concatenate([fwd, bwd], axis=-1).astype(F32)
    return jnp.broadcast_to(v[..., None], v.shape + (LANES,))


def _head_expanders():
    j = jnp.arange(LANES)[None, :, None]
    k = jnp.arange(N_EXPAND)[:, None, None]
    head = jnp.arange(SSD_HEADS * SSD_HEADDIM)[None, None, :] // SSD_HEADDIM
    half = N_EXPAND * SSD_HEADS
    return ((j == k * SSD_HEADS + head) | (j == half + k * SSD_HEADS + head)).astype(BF16)


def kernel(x, w_in, conv_xbc_w, conv_xbc_b, dt_bias_f, dt_bias_b, a_log_f, a_log_b, d_skip,
           ssd_norm_w, sc_conv_w, w_out, norm1_w, norm2_w, w_ffn_up, ffn_conv_w, ffn_conv_b,
           w_ffn_down, final_norm_w):
    depth = w_in.shape[0]
    d_ssd = SSD_HEADS * SSD_HEADDIM
    o_dt = d_ssd + d_ssd + 2 * SSD_GROUPS * SSD_STATE
    o_sc = o_dt + 2 * SSD_HEADS
    length = x.shape[1]
    w_in_all = jnp.concatenate(
        [w_in[..., :o_dt], w_in[..., o_sc:],
         jnp.pad(w_in[..., o_dt:o_sc], ((0, 0), (0, 0), (0, LANES - 2 * SSD_HEADS)))], axis=-1).astype(BF16)
    w_out_all = w_out.astype(BF16)
    w_up_all = w_ffn_up.astype(BF16)
    w_down_all = w_ffn_down.astype(BF16)
    rows3 = lambda v: v.astype(F32)[:, None, :]
    norm1, norm2, gnorm = rows3(norm1_w), rows3(norm2_w), rows3(ssd_norm_w)
    conv_b, ffn_b = rows3(conv_xbc_b), rows3(ffn_conv_b)
    dt_bias = _lane_rows(dt_bias_f, dt_bias_b)
    a_log = _lane_rows(a_log_f, a_log_b)
    d_expand = rows3(jnp.repeat(d_skip, SSD_HEADDIM, axis=-1))
    final_w = final_norm_w.astype(F32).reshape(1, -1)
    expanders = _head_expanders()

    h = x
    for l in range(depth):
        z, xbc, y_sc, rows, cols, packed = _inproj(
            h, l, norm1, w_in_all, conv_xbc_w.astype(F32), conv_b, sc_conv_w.astype(F32), dt_bias, a_log,
            tile=min(INPROJ_TILE, length))
        y_f, y_b = _ssd(xbc, rows, cols, packed, l, d_expand, expanders)
        h = _mixout(h, y_f, y_b, z, y_sc, l, gnorm, w_out_all, tile=min(MIXOUT_TILE, length))
        h = _ffn(h, l, norm2, w_up_all, ffn_conv_w.astype(F32), ffn_b, w_down_all, final_w,
                 tile=min(FFN_TILE, length), final_norm=(l == depth - 1))
    return h
```

```python
import functools

import jax
import jax.numpy as jnp
from jax import lax
from jax.experimental import pallas as pl
from jax.experimental.pallas import tpu as pltpu

F32 = jnp.float32
BF16 = jnp.bfloat16

RMS_EPS = 1e-5
SSD_HEADDIM = 64
SSD_GROUPS = 4
SSD_STATE = 128
CHUNK = 128
HEADS_PER_GROUP = 4
SSD_HEADS = SSD_GROUPS * HEADS_PER_GROUP
GROUP_WIDTH = HEADS_PER_GROUP * SSD_HEADDIM

LANES = 128
F32_SUBLANES = 8
BF16_SUBLANES = 16
HALO = BF16_SUBLANES
VMEM_LIMIT_BYTES = 56 * 1024 * 1024


def _dot(a, b):
    return jnp.dot(a, b, preferred_element_type=F32)


def _dot_nt(a, b):
    return lax.dot_general(a, b, (((1,), (1,)), ((), ())), preferred_element_type=F32)


def _dot_tn(a, b):
    return lax.dot_general(a, b, (((0,), (0,)), ((), ())), preferred_element_type=F32)


def _silu(x):
    return x * (1.0 / (1.0 + jnp.exp(-x)))


def _rms(v, w):
    ms = jnp.mean(v * v, axis=-1, keepdims=True)
    return v * lax.rsqrt(ms + RMS_EPS) * w


def _const_spec(shape):
    zeros = (0,) * len(shape)
    return pl.BlockSpec(shape, lambda *_: zeros, pipeline_mode=pl.Buffered(1))


def _layer_spec(stacked, layer, cols=None, col_block=0):
    _, rows, width = stacked.shape
    return pl.BlockSpec((None, rows, width if cols is None else cols),
                        lambda *_: (layer, 0, col_block), pipeline_mode=pl.Buffered(1))


def _halo_lhs(prev_ref, main_ref, next_ref, norm_w, lhs_ref):
    t = pl.program_id(1)
    last = pl.num_programs(1) - 1
    d = main_ref.shape[-1]
    prev = _rms(prev_ref[0], norm_w) * (t > 0).astype(F32)
    nxt = _rms(next_ref[0], norm_w) * (t < last).astype(F32)
    main = _rms(main_ref[0], norm_w)
    zeros = jnp.zeros((HALO - F32_SUBLANES, d), F32)
    lhs_ref[...] = jnp.concatenate([zeros, prev, main, nxt, zeros], axis=0).astype(BF16)


def _conv3(p_ref, w_ref, rows):
    return (p_ref[HALO - 1:HALO - 1 + rows, :] * w_ref[0:1, :]
            + p_ref[HALO:HALO + rows, :] * w_ref[1:2, :]
            + p_ref[HALO + 1:HALO + 1 + rows, :] * w_ref[2:3, :])


IN_COL_BLOCK = 512
INPROJ_TILE = 1024


def _inproj_kernel(hp_ref, h_ref, hn_ref, nw_ref, wz_ref, wxa_ref, wxb_ref, wscb_ref, wscc_ref,
                   wscv_ref, wdt_ref, cw_ref, cb_ref, scw_ref, bias_ref, alog_ref,
                   z_out, xbc_out, ysc_out, rows_out, cols_out, packed_out, lhs_ref, p_ref):
    rows = h_ref.shape[1]
    _halo_lhs(hp_ref, h_ref, hn_ref, nw_ref[...], lhs_ref)
    centre = lhs_ref[HALO:HALO + rows, :]
    full = lhs_ref[...]

    _ssd_scalars(_dot(centre, wdt_ref[...]), bias_ref, alog_ref, rows_out, cols_out, packed_out)
    for c in range(0, wz_ref.shape[1], IN_COL_BLOCK):
        cs = slice(c, c + IN_COL_BLOCK)
        z_out[0, :, cs] = _dot(centre, wz_ref[:, cs]).astype(z_out.dtype)
    slot = 0
    half = wxa_ref.shape[1]
    for c in range(0, 2 * half, IN_COL_BLOCK):
        cs = slice(c, c + IN_COL_BLOCK)
        w_ref = wxa_ref if c < half else wxb_ref
        pbuf = p_ref.at[slot]
        slot = 1 - slot
        pbuf[...] = _dot(full, w_ref[:, c % half:c % half + IN_COL_BLOCK])
        conv = _conv3(pbuf, cw_ref.at[:, cs], rows) + cb_ref[:, cs]
        xbc_out[0, :, cs] = _silu(conv).astype(xbc_out.dtype)
    for c in range(0, wscb_ref.shape[1], IN_COL_BLOCK):
        cs = slice(c, c + IN_COL_BLOCK)
        pbuf = p_ref.at[slot]
        slot = 1 - slot
        pbuf[...] = _dot(full, wscc_ref[:, cs]) * _dot(full, wscv_ref[:, cs])
        gate = _dot(centre, wscb_ref[:, cs])
        ysc_out[0, :, cs] = (gate * _conv3(pbuf, scw_ref.at[:, cs], rows)).astype(ysc_out.dtype)


def _inproj(h, layer, norm_w, w_in, conv_w, conv_b, sc_w, dt_bias, a_log, *, tile):
    bsz, length, d = h.shape
    n_tiles = length // tile
    blk8 = tile // F32_SUBLANES
    last8 = length // F32_SUBLANES - 1
    grid = (bsz, n_tiles)
    row_spec = lambda width: pl.BlockSpec((1, tile, width), lambda b, t: (b, t, 0))
    w_block = lambda j: _layer_spec(w_in, layer, cols=d, col_block=j)
    in_specs = [
        pl.BlockSpec((1, F32_SUBLANES, d), lambda b, t: (b, jnp.maximum(t * blk8 - 1, 0), 0)),
        row_spec(d),
        pl.BlockSpec((1, F32_SUBLANES, d), lambda b, t: (b, jnp.minimum((t + 1) * blk8, last8), 0)),
        _layer_spec(norm_w, layer),
        w_block(0), w_block(1), w_block(2), w_block(3), w_block(4), w_block(5),
        _layer_spec(w_in, layer, cols=LANES, col_block=6 * d // LANES),
        _layer_spec(conv_w, layer), _layer_spec(conv_b, layer), _layer_spec(sc_w, layer),
        _layer_spec(dt_bias, layer), _layer_spec(a_log, layer),
    ]
    out_shape = (
        jax.ShapeDtypeStruct((bsz, length, d), BF16),
        jax.ShapeDtypeStruct((bsz, length, 2 * d), BF16),
        jax.ShapeDtypeStruct((bsz, length, d), BF16),
        jax.ShapeDtypeStruct((bsz, length // CHUNK, 3 * SSD_HEADS, CHUNK), F32),
        jax.ShapeDtypeStruct((bsz, length, LANES), F32),
        jax.ShapeDtypeStruct((bsz, length, LANES), BF16),
    )
    out_specs = (row_spec(d), row_spec(2 * d), row_spec(d),
                 pl.BlockSpec((1, tile // CHUNK, 3 * SSD_HEADS, CHUNK), lambda b, t: (b, t, 0, 0)),
                 row_spec(LANES), row_spec(LANES))
    return pl.pallas_call(
        _inproj_kernel, grid=grid, in_specs=in_specs, out_specs=out_specs, out_shape=out_shape,
        scratch_shapes=[pltpu.VMEM((tile + 2 * HALO, d), BF16),
                        pltpu.VMEM((2, tile + 2 * HALO, IN_COL_BLOCK), F32)],
        compiler_params=pltpu.CompilerParams(
            dimension_semantics=("parallel", "parallel"), vmem_limit_bytes=VMEM_LIMIT_BYTES),
        name="inproj",
    )(h, h, h, norm_w, w_in, w_in, w_in, w_in, w_in, w_in, w_in, conv_w, conv_b, sc_w, dt_bias, a_log)


N_EXPAND = 4


def _softplus(x):
    return jnp.maximum(x, 0.0) + jnp.log(1.0 + jnp.exp(-jnp.abs(x)))


def _cumsum_lanes(a):
    lane = lax.broadcasted_iota(jnp.int32, a.shape, 1)
    shift = 1
    while shift < a.shape[1]:
        a = a + jnp.where(lane >= shift, pltpu.roll(a, shift, 1), 0.0)
        shift *= 2
    return a


def _pad_rows(v, rows):
    return jnp.concatenate([v, jnp.zeros((rows - v.shape[0], v.shape[1]), v.dtype)], axis=0)


def _ssd_scalars(dt_raw, bias_ref, alog_ref, rows_out, cols_out, packed_out):
    nh = SSD_HEADS
    n_chunks = dt_raw.shape[0] // CHUNK
    raw = jnp.concatenate(
        [dt_raw[c * CHUNK:(c + 1) * CHUNK, :].T[0:2 * nh] for c in range(n_chunks)], axis=0)
    bias = jnp.concatenate([bias_ref[...]] * n_chunks, axis=0)
    neg_a = -jnp.exp(jnp.concatenate([alog_ref[...]] * n_chunks, axis=0))
    dt = _softplus(raw + bias)
    a = dt * neg_a
    acs = _cumsum_lanes(a)
    ecs = acs - a
    tot = jnp.broadcast_to(acs[:, CHUNK - 1:CHUNK], acs.shape)
    log_dt = jnp.log(dt)
    grow_f = jnp.exp(acs)
    xscale_f = dt * jnp.exp(tot - acs)
    grow_b = jnp.exp(tot - ecs)
    xscale_b = dt * jnp.exp(ecs)
    for c in range(n_chunks):
        fwd = slice(2 * nh * c, 2 * nh * c + nh)
        bwd = slice(2 * nh * c + nh, 2 * nh * (c + 1))
        rows_out[0, c] = jnp.concatenate(
            [acs[fwd] - log_dt[fwd],
             ecs[bwd] + log_dt[bwd],
             jnp.log(dt[fwd] + dt[bwd])], axis=0)
        tok = slice(c * CHUNK, (c + 1) * CHUNK)
        cols_out[0, tok, :] = _pad_rows(jnp.concatenate([acs[fwd], ecs[bwd]], axis=0), CHUNK).T
        wide = jnp.concatenate([grow_f[fwd], xscale_f[fwd], grow_b[bwd], xscale_b[bwd]], axis=0)
        wide_hi = wide.astype(BF16).astype(F32)
        packed_out[0, tok, :] = jnp.concatenate([wide_hi, wide - wide_hi], axis=0).T.astype(BF16)


SSD_STEP_CHUNKS = 8


def _ssd_kernel(xf_ref, xb_ref, rows_ref, cols_ref, pf_ref, pb_ref, dexp_ref, e_ref,
                yf_ref, yb_ref, *state_refs):
    hf_ref = state_refs[:SSD_GROUPS]
    hb_ref = state_refs[SSD_GROUPS:]
    d_ssd = SSD_HEADS * SSD_HEADDIM
    b_off = d_ssd
    c_off = d_ssd + SSD_GROUPS * SSD_STATE
    nh = SSD_HEADS

    @pl.when(pl.program_id(1) == 0)
    def _():
        for ref in state_refs:
            ref[...] = jnp.zeros_like(ref)

    row_i = lax.broadcasted_iota(jnp.int32, (CHUNK, CHUNK), 0)
    col_i = lax.broadcasted_iota(jnp.int32, (CHUNK, CHUNK), 1)
    below = row_i > col_i
    above = row_i < col_i
    head_lane = lax.broadcasted_iota(jnp.int32, (CHUNK, GROUP_WIDTH), 1) // SSD_HEADDIM

    for k in range(SSD_STEP_CHUNKS):
        tf = slice(k * CHUNK, (k + 1) * CHUNK)
        tb = slice((SSD_STEP_CHUNKS - 1 - k) * CHUNK, (SSD_STEP_CHUNKS - k) * CHUNK)
        row_f = rows_ref[0, k, 0:nh, :]
        row_b = rows_ref[0, k, nh:2 * nh, :]
        row_d = rows_ref[0, k, 2 * nh:3 * nh, :]
        cols = cols_ref[0, tf, :]
        packed_f = pf_ref[0, tf, :]
        packed_b = pb_ref[0, tb, :]

        for g in range(SSD_GROUPS):
            gs = slice(g * GROUP_WIDTH, (g + 1) * GROUP_WIDTH)
            bs = slice(b_off + g * SSD_STATE, b_off + (g + 1) * SSD_STATE)
            cs = slice(c_off + g * SSD_STATE, c_off + (g + 1) * SSD_STATE)
            grow_f = _dot(packed_f, e_ref[0, :, gs])
            xscale_f = _dot(packed_f, e_ref[1, :, gs])
            grow_b = _dot(packed_b, e_ref[2, :, gs])
            xscale_b = _dot(packed_b, e_ref[3, :, gs])
            hdecay_f = grow_f[CHUNK - 1:CHUNK, :]
            hdecay_b = grow_b[0:1, :]

            xs = xf_ref[0, tf, gs]
            bm = xf_ref[0, tf, bs]
            cm = xf_ref[0, tf, cs]
            scores = _dot_nt(cm, bm)
            w_heads = []
            x_blocks = []
            for r in range(HEADS_PER_GROUP):
                hd = g * HEADS_PER_GROUP + r
                expo = jnp.where(below, cols[:, hd:hd + 1] - row_f[hd:hd + 1, :],
                                 jnp.where(above, row_b[hd:hd + 1, :] - cols[:, nh + hd:nh + hd + 1],
                                           row_d[hd:hd + 1, :]))
                w_heads.append((scores * jnp.exp(expo)).astype(BF16))
                x_blocks.append(jnp.where(head_lane == r, xs, jnp.zeros_like(xs)))
            xs32 = xs.astype(F32)
            y = _dot(jnp.concatenate(w_heads, axis=1), jnp.concatenate(x_blocks, axis=0))
            y = y + xs32 * dexp_ref[:, gs] + _dot(cm, hf_ref[g][...].astype(BF16)) * grow_f
            yf_ref[0, tf, gs] = y.astype(yf_ref.dtype)
            xt = (xs32 * xscale_f).astype(BF16)
            hf_ref[g][...] = hf_ref[g][...] * hdecay_f + _dot_tn(bm, xt)

            xs = xb_ref[0, tb, gs]
            bm = xb_ref[0, tb, bs]
            cm = xb_ref[0, tb, cs]
            yb_ref[0, tb, gs] = (_dot(cm, hb_ref[g][...].astype(BF16)) * grow_b).astype(yb_ref.dtype)
            xt = (xs.astype(F32) * xscale_b).astype(BF16)
            hb_ref[g][...] = hb_ref[g][...] * hdecay_b + _dot_tn(bm, xt)


def _ssd(xbc, rows, cols, packed, layer, d_expand, expanders):
    bsz, length, width = xbc.shape
    span = SSD_STEP_CHUNKS * CHUNK
    n_steps = length // span
    d_ssd = SSD_HEADS * SSD_HEADDIM
    fwd = lambda b, i: (b, i, 0)
    bwd = lambda b, i: (b, n_steps - 1 - i, 0)
    in_specs = [
        pl.BlockSpec((1, span, width), fwd),
        pl.BlockSpec((1, span, width), bwd),
        pl.BlockSpec((1, SSD_STEP_CHUNKS, 3 * SSD_HEADS, CHUNK), lambda b, i: (b, i, 0, 0)),
        pl.BlockSpec((1, span, LANES), fwd),
        pl.BlockSpec((1, span, LANES), fwd),
        pl.BlockSpec((1, span, LANES), bwd),
        _layer_spec(d_expand, layer), _const_spec(expanders.shape),
    ]
    out_shape = (jax.ShapeDtypeStruct((bsz, length, d_ssd), BF16),
                 jax.ShapeDtypeStruct((bsz, length, d_ssd), BF16))
    out_specs = (pl.BlockSpec((1, span, d_ssd), fwd), pl.BlockSpec((1, span, d_ssd), bwd))
    state = pltpu.VMEM((SSD_STATE, GROUP_WIDTH), F32)
    return pl.pallas_call(
        _ssd_kernel, grid=(bsz, n_steps), in_specs=in_specs, out_specs=out_specs,
        out_shape=out_shape, scratch_shapes=[state] * (2 * SSD_GROUPS),
        compiler_params=pltpu.CompilerParams(
            dimension_semantics=("parallel", "arbitrary"), vmem_limit_bytes=VMEM_LIMIT_BYTES),
        name="ssd",
    )(xbc, xbc, rows, cols, packed, packed, d_expand, expanders)


MIXOUT_TILE = 1024
MIXOUT_ROW_BLOCK = 256


def _mixout_kernel(h_ref, yf_ref, yb_ref, z_ref, ysc_ref, gw_ref, wo_ref, out_ref):
    d_ssd = yf_ref.shape[-1]
    rows = h_ref.shape[1]
    for r in range(0, rows, MIXOUT_ROW_BLOCK):
        rs = slice(r, min(r + MIXOUT_ROW_BLOCK, rows))
        y = (yf_ref[0, rs, :].astype(F32) + yb_ref[0, rs, :].astype(F32)) * _silu(z_ref[0, rs, :].astype(F32))
        normed = []
        for g in range(SSD_GROUPS):
            gs = slice(g * GROUP_WIDTH, (g + 1) * GROUP_WIDTH)
            normed.append(_rms(y[:, gs], gw_ref[:, gs]).astype(BF16))
        yn = jnp.concatenate(normed, axis=1)
        out = _dot(yn, wo_ref[0:d_ssd, :]) + _dot(ysc_ref[0, rs, :], wo_ref[d_ssd:, :])
        out_ref[0, rs, :] = h_ref[0, rs, :] + out


def _mixout(h, y_f, y_b, z, y_sc, layer, gnorm_w, w_out, *, tile):
    bsz, length, d = h.shape
    row_spec = lambda width: pl.BlockSpec((1, tile, width), lambda b, t: (b, t, 0))
    in_specs = [row_spec(d), row_spec(y_f.shape[-1]), row_spec(y_b.shape[-1]),
                row_spec(z.shape[-1]), row_spec(y_sc.shape[-1]),
                _layer_spec(gnorm_w, layer), _layer_spec(w_out, layer)]
    return pl.pallas_call(
        _mixout_kernel, grid=(bsz, length // tile), in_specs=in_specs, out_specs=row_spec(d),
        out_shape=jax.ShapeDtypeStruct(h.shape, F32),
        compiler_params=pltpu.CompilerParams(
            dimension_semantics=("parallel", "parallel"), vmem_limit_bytes=VMEM_LIMIT_BYTES),
        name="mixout",
    )(h, y_f, y_b, z, y_sc, gnorm_w, w_out)


FF_COL_BLOCK = 256
FFN_TILE = 1024


def _strand_conv3(p_ref, w_ref, rows):
    sub = lax.broadcasted_iota(jnp.int32, (F32_SUBLANES, p_ref.shape[1]), 0)
    width = p_ref.shape[1]
    before = jnp.broadcast_to(p_ref[rows:rows + 1, :], (F32_SUBLANES, width))
    after = jnp.broadcast_to(p_ref[rows + 1:rows + 2, :], (F32_SUBLANES, width))
    head = jnp.where(sub == 0, before, pltpu.roll(p_ref[rows - F32_SUBLANES:rows, :], 1, 0))
    tail = jnp.where(sub == F32_SUBLANES - 1, after,
                     pltpu.roll(p_ref[0:F32_SUBLANES, :], F32_SUBLANES - 1, 0))
    prev = jnp.concatenate([head, p_ref[0:rows - F32_SUBLANES, :]], axis=0)
    nxt = jnp.concatenate([p_ref[F32_SUBLANES:rows, :], tail], axis=0)
    return prev * w_ref[0:1, :] + p_ref[0:rows, :] * w_ref[1:2, :] + nxt * w_ref[2:3, :]


def _strand_blocks(strand, d):
    return [(s, w, j) for s in range(F32_SUBLANES) for w in range(strand // F32_SUBLANES)
            for j in range(d // LANES)]


def _ffn_kernel(hp_ref, h_ref, hn_ref, nw_ref, wup_ref, cw_ref, cb_ref, wdn_ref, fw_ref,
                out_ref, hs_ref, lhs_ref, act_ref, p_ref, *, final_norm):
    rows = h_ref.shape[1]
    d = h_ref.shape[2]
    strand = rows // F32_SUBLANES
    d_ff = wdn_ref.shape[0]
    t = pl.program_id(1)
    last = pl.num_programs(1) - 1
    norm_w = nw_ref[...]

    for s, w, j in _strand_blocks(strand, d):
        tok = strand * s + F32_SUBLANES * w
        hs_ref[j, pl.ds(F32_SUBLANES * F32_SUBLANES * w + s, F32_SUBLANES, stride=F32_SUBLANES), :] = (
            h_ref[0, tok:tok + F32_SUBLANES, j * LANES:(j + 1) * LANES])
    h_strand = jnp.concatenate([hs_ref[j] for j in range(d // LANES)], axis=1)
    before = _rms(hp_ref[0], norm_w)[F32_SUBLANES - 1:F32_SUBLANES] * (t > 0).astype(F32)
    after = _rms(hn_ref[0], norm_w)[0:1] * (t < last).astype(F32)
    halo_row = lax.broadcasted_iota(jnp.int32, (HALO, d), 0)
    halo = jnp.where(halo_row == 0, jnp.broadcast_to(before, (HALO, d)),
                     jnp.where(halo_row == 1, jnp.broadcast_to(after, (HALO, d)), 0.0))
    lhs_ref[...] = jnp.concatenate([_rms(h_strand, norm_w), halo], axis=0).astype(BF16)
    full = lhs_ref[...]

    slot = 0
    for c in range(0, d_ff, FF_COL_BLOCK):
        gs = slice(c, c + FF_COL_BLOCK)
        us = slice(d_ff + c, d_ff + c + FF_COL_BLOCK)
        gbuf = p_ref.at[slot, 0]
        ubuf = p_ref.at[slot, 1]
        slot = 1 - slot
        gbuf[...] = _dot(full, wup_ref[:, gs])
        ubuf[...] = _dot(full, wup_ref[:, us])
        gate = _strand_conv3(gbuf, cw_ref.at[:, gs], rows) + cb_ref[:, gs]
        up = _strand_conv3(ubuf, cw_ref.at[:, us], rows) + cb_ref[:, us]
        act_ref[:, gs] = (_silu(gate) * up).astype(BF16)
    out = h_strand + _dot(act_ref[...], wdn_ref[...])
    if final_norm:
        out = _rms(out, fw_ref[...])
    for j in range(d // LANES):
        hs_ref[j] = out[:, j * LANES:(j + 1) * LANES]
    for s, w, j in _strand_blocks(strand, d):
        tok = strand * s + F32_SUBLANES * w
        out_ref[0, tok:tok + F32_SUBLANES, j * LANES:(j + 1) * LANES] = (
            hs_ref[j, pl.ds(F32_SUBLANES * F32_SUBLANES * w + s, F32_SUBLANES, stride=F32_SUBLANES), :])


def _ffn(h, layer, norm_w, w_up, conv_w, conv_b, w_down, final_w, *, tile, final_norm):
    bsz, length, d = h.shape
    d_ff = w_down.shape[1]
    blk8 = tile // F32_SUBLANES
    last8 = length // F32_SUBLANES - 1
    row_spec = pl.BlockSpec((1, tile, d), lambda b, t: (b, t, 0))
    in_specs = [
        pl.BlockSpec((1, F32_SUBLANES, d), lambda b, t: (b, jnp.maximum(t * blk8 - 1, 0), 0)),
        row_spec,
        pl.BlockSpec((1, F32_SUBLANES, d), lambda b, t: (b, jnp.minimum((t + 1) * blk8, last8), 0)),
        _layer_spec(norm_w, layer), _layer_spec(w_up, layer), _layer_spec(conv_w, layer),
        _layer_spec(conv_b, layer), _layer_spec(w_down, layer), _const_spec(final_w.shape),
    ]
    return pl.pallas_call(
        functools.partial(_ffn_kernel, final_norm=final_norm),
        grid=(bsz, length // tile), in_specs=in_specs, out_specs=row_spec,
        out_shape=jax.ShapeDtypeStruct(h.shape, F32),
        scratch_shapes=[pltpu.VMEM((d // LANES, tile, LANES), F32), pltpu.VMEM((tile + HALO, d), BF16),
                        pltpu.VMEM((tile, d_ff), BF16),
                        pltpu.VMEM((2, 2, tile + HALO, FF_COL_BLOCK), F32)],
        compiler_params=pltpu.CompilerParams(
            dimension_semantics=("parallel", "parallel"), vmem_limit_bytes=VMEM_LIMIT_BYTES),
        name="ffn",
    )(h, h, h, norm_w, w_up, conv_w, conv_b, w_down, final_w)


def _lane_rows(fwd, bwd):
    v = jnp.concatenate([fwd, bwd], axis=-1).astype(F32)
    return jnp.broadcast_to(v[..., None], v.shape + (LANES,))


def _head_expanders():
    j = jnp.arange(LANES)[None, :, None]
    k = jnp.arange(N_EXPAND)[:, None, None]
    head = jnp.arange(SSD_HEADS * SSD_HEADDIM)[None, None, :] // SSD_HEADDIM
    half = N_EXPAND * SSD_HEADS
    return ((j == k * SSD_HEADS + head) | (j == half + k * SSD_HEADS + head)).astype(BF16)


def kernel(x, w_in, conv_xbc_w, conv_xbc_b, dt_bias_f, dt_bias_b, a_log_f, a_log_b, d_skip,
           ssd_norm_w, sc_conv_w, w_out, norm1_w, norm2_w, w_ffn_up, ffn_conv_w, ffn_conv_b,
           w_ffn_down, final_norm_w):
    depth = w_in.shape[0]
    d_ssd = SSD_HEADS * SSD_HEADDIM
    o_dt = d_ssd + d_ssd + 2 * SSD_GROUPS * SSD_STATE
    o_sc = o_dt + 2 * SSD_HEADS
    length = x.shape[1]
    w_in_all = jnp.concatenate(
        [w_in[..., :o_dt], w_in[..., o_sc:],
         jnp.pad(w_in[..., o_dt:o_sc], ((0, 0), (0, 0), (0, LANES - 2 * SSD_HEADS)))], axis=-1).astype(BF16)
    w_out_all = w_out.astype(BF16)
    w_up_all = w_ffn_up.astype(BF16)
    w_down_all = w_ffn_down.astype(BF16)
    rows3 = lambda v: v.astype(F32)[:, None, :]
    norm1, norm2, gnorm = rows3(norm1_w), rows3(norm2_w), rows3(ssd_norm_w)
    conv_b, ffn_b = rows3(conv_xbc_b), rows3(ffn_conv_b)
    dt_bias = _lane_rows(dt_bias_f, dt_bias_b)
    a_log = _lane_rows(a_log_f, a_log_b)
    d_expand = rows3(jnp.repeat(d_skip, SSD_HEADDIM, axis=-1))
    final_w = final_norm_w.astype(F32).reshape(1, -1)
    expanders = _head_expanders()

    h = x
    for l in range(depth):
        z, xbc, y_sc, rows, cols, packed = _inproj(
            h, l, norm1, w_in_all, conv_xbc_w.astype(F32), conv_b, sc_conv_w.astype(F32), dt_bias, a_log,
            tile=min(INPROJ_TILE, length))
        y_f, y_b = _ssd(xbc, rows, cols, packed, l, d_expand, expanders)
        h = _mixout(h, y_f, y_b, z, y_sc, l, gnorm, w_out_all, tile=min(MIXOUT_TILE, length))
        h = _ffn(h, l, norm2, w_up_all, ffn_conv_w.astype(F32), ffn_b, w_down_all, final_w,
                 tile=min(FFN_TILE, length), final_norm=(l == depth - 1))
    return h
```

```python
import functools

import jax
import jax.numpy as jnp
from jax import lax
from jax.experimental import pallas as pl
from jax.experimental.pallas import tpu as pltpu

F32 = jnp.float32
BF16 = jnp.bfloat16

RMS_EPS = 1e-5
SSD_HEADDIM = 64
SSD_GROUPS = 4
SSD_STATE = 128
CHUNK = 128
HEADS_PER_GROUP = 4
SSD_HEADS = SSD_GROUPS * HEADS_PER_GROUP
GROUP_WIDTH = HEADS_PER_GROUP * SSD_HEADDIM

LANES = 128
F32_SUBLANES = 8
BF16_SUBLANES = 16
HALO = BF16_SUBLANES
VMEM_LIMIT_BYTES = 56 * 1024 * 1024


def _dot(a, b):
    return jnp.dot(a, b, preferred_element_type=F32)


def _dot_nt(a, b):
    return lax.dot_general(a, b, (((1,), (1,)), ((), ())), preferred_element_type=F32)


def _dot_tn(a, b):
    return lax.dot_general(a, b, (((0,), (0,)), ((), ())), preferred_element_type=F32)


def _silu(x):
    return x * (1.0 / (1.0 + jnp.exp(-x)))


def _rms(v, w):
    ms = jnp.mean(v * v, axis=-1, keepdims=True)
    return v * lax.rsqrt(ms + RMS_EPS) * w


def _const_spec(shape):
    zeros = (0,) * len(shape)
    return pl.BlockSpec(shape, lambda *_: zeros, pipeline_mode=pl.Buffered(1))


def _layer_spec(stacked, layer, cols=None, col_block=0):
    _, rows, width = stacked.shape
    return pl.BlockSpec((None, rows, width if cols is None else cols),
                        lambda *_: (layer, 0, col_block), pipeline_mode=pl.Buffered(1))


def _halo_lhs(prev_ref, main_ref, next_ref, norm_w, lhs_ref):
    t = pl.program_id(1)
    last = pl.num_programs(1) - 1
    d = main_ref.shape[-1]
    prev = _rms(prev_ref[0], norm_w) * (t > 0).astype(F32)
    nxt = _rms(next_ref[0], norm_w) * (t < last).astype(F32)
    main = _rms(main_ref[0], norm_w)
    zeros = jnp.zeros((HALO - F32_SUBLANES, d), F32)
    lhs_ref[...] = jnp.concatenate([zeros, prev, main, nxt, zeros], axis=0).astype(BF16)


def _conv3(p_ref, w_ref, rows):
    return (p_ref[HALO - 1:HALO - 1 + rows, :] * w_ref[0:1, :]
            + p_ref[HALO:HALO + rows, :] * w_ref[1:2, :]
            + p_ref[HALO + 1:HALO + 1 + rows, :] * w_ref[2:3, :])


IN_COL_BLOCK = 512
INPROJ_TILE = 1024


def _inproj_kernel(hp_ref, h_ref, hn_ref, nw_ref, wz_ref, wxa_ref, wxb_ref, wscb_ref, wscc_ref,
                   wscv_ref, wdt_ref, cw_ref, cb_ref, scw_ref, bias_ref, alog_ref,
                   z_out, xbc_out, ysc_out, rows_out, cols_out, packed_out, lhs_ref, p_ref):
    rows = h_ref.shape[1]
    _halo_lhs(hp_ref, h_ref, hn_ref, nw_ref[...], lhs_ref)
    centre = lhs_ref[HALO:HALO + rows, :]
    full = lhs_ref[...]

    _ssd_scalars(_dot(centre, wdt_ref[...]), bias_ref, alog_ref, rows_out, cols_out, packed_out)
    for c in range(0, wz_ref.shape[1], IN_COL_BLOCK):
        cs = slice(c, c + IN_COL_BLOCK)
        z_out[0, :, cs] = _dot(centre, wz_ref[:, cs]).astype(z_out.dtype)
    slot = 0
    half = wxa_ref.shape[1]
    for c in range(0, 2 * half, IN_COL_BLOCK):
        cs = slice(c, c + IN_COL_BLOCK)
        w_ref = wxa_ref if c < half else wxb_ref
        pbuf = p_ref.at[slot]
        slot = 1 - slot
        pbuf[...] = _dot(full, w_ref[:, c % half:c % half + IN_COL_BLOCK])
        conv = _conv3(pbuf, cw_ref.at[:, cs], rows) + cb_ref[:, cs]
        xbc_out[0, :, cs] = _silu(conv).astype(xbc_out.dtype)
    for c in range(0, wscb_ref.shape[1], IN_COL_BLOCK):
        cs = slice(c, c + IN_COL_BLOCK)
        pbuf = p_ref.at[slot]
        slot = 1 - slot
        pbuf[...] = _dot(full, wscc_ref[:, cs]) * _dot(full, wscv_ref[:, cs])
        gate = _dot(centre, wscb_ref[:, cs])
        ysc_out[0, :, cs] = (gate * _conv3(pbuf, scw_ref.at[:, cs], rows)).astype(ysc_out.dtype)


def _inproj(h, layer, norm_w, w_head, w_tail, conv_w, conv_b, sc_w, dt_bias, a_log, *, tile):
    bsz, length, d = h.shape
    n_tiles = length // tile
    blk8 = tile // F32_SUBLANES
    last8 = length // F32_SUBLANES - 1
    grid = (bsz, n_tiles)
    row_spec = lambda width: pl.BlockSpec((1, tile, width), lambda b, t: (b, t, 0))
    head = lambda j: _layer_spec(w_head, layer, cols=d, col_block=j)
    tail = lambda j: _layer_spec(w_tail, layer, cols=d, col_block=j)
    in_specs = [
        pl.BlockSpec((1, F32_SUBLANES, d), lambda b, t: (b, jnp.maximum(t * blk8 - 1, 0), 0)),
        row_spec(d),
        pl.BlockSpec((1, F32_SUBLANES, d), lambda b, t: (b, jnp.minimum((t + 1) * blk8, last8), 0)),
        _layer_spec(norm_w, layer),
        head(0), head(1), head(2), tail(0), tail(1), tail(2),
        _layer_spec(w_tail, layer, cols=LANES, col_block=3 * d // LANES),
        _layer_spec(conv_w, layer), _layer_spec(conv_b, layer), _layer_spec(sc_w, layer),
        _layer_spec(dt_bias, layer), _layer_spec(a_log, layer),
    ]
    out_shape = (
        jax.ShapeDtypeStruct((bsz, length, d), BF16),
        jax.ShapeDtypeStruct((bsz, length, 2 * d), BF16),
        jax.ShapeDtypeStruct((bsz, length, d), BF16),
        jax.ShapeDtypeStruct((bsz, length // CHUNK, 3 * SSD_HEADS, CHUNK), F32),
        jax.ShapeDtypeStruct((bsz, length, LANES), F32),
        jax.ShapeDtypeStruct((bsz, length, LANES), BF16),
    )
    out_specs = (row_spec(d), row_spec(2 * d), row_spec(d),
                 pl.BlockSpec((1, tile // CHUNK, 3 * SSD_HEADS, CHUNK), lambda b, t: (b, t, 0, 0)),
                 row_spec(LANES), row_spec(LANES))
    return pl.pallas_call(
        _inproj_kernel, grid=grid, in_specs=in_specs, out_specs=out_specs, out_shape=out_shape,
        scratch_shapes=[pltpu.VMEM((tile + 2 * HALO, d), BF16),
                        pltpu.VMEM((2, tile + 2 * HALO, IN_COL_BLOCK), F32)],
        compiler_params=pltpu.CompilerParams(
            dimension_semantics=("parallel", "parallel"), vmem_limit_bytes=VMEM_LIMIT_BYTES),
        name="inproj",
    )(h, h, h, norm_w, w_head, w_head, w_head, w_tail, w_tail, w_tail, w_tail, conv_w, conv_b, sc_w,
      dt_bias, a_log)


N_EXPAND = 4


def _softplus(x):
    return jnp.maximum(x, 0.0) + jnp.log(1.0 + jnp.exp(-jnp.abs(x)))


def _cumsum_lanes(a):
    lane = lax.broadcasted_iota(jnp.int32, a.shape, 1)
    shift = 1
    while shift < a.shape[1]:
        a = a + jnp.where(lane >= shift, pltpu.roll(a, shift, 1), 0.0)
        shift *= 2
    return a


def _pad_rows(v, rows):
    return jnp.concatenate([v, jnp.zeros((rows - v.shape[0], v.shape[1]), v.dtype)], axis=0)


def _ssd_scalars(dt_raw, bias_ref, alog_ref, rows_out, cols_out, packed_out):
    nh = SSD_HEADS
    n_chunks = dt_raw.shape[0] // CHUNK
    raw = jnp.concatenate(
        [dt_raw[c * CHUNK:(c + 1) * CHUNK, :].T[0:2 * nh] for c in range(n_chunks)], axis=0)
    bias = jnp.concatenate([bias_ref[...]] * n_chunks, axis=0)
    neg_a = -jnp.exp(jnp.concatenate([alog_ref[...]] * n_chunks, axis=0))
    dt = _softplus(raw + bias)
    a = dt * neg_a
    acs = _cumsum_lanes(a)
    ecs = acs - a
    tot = jnp.broadcast_to(acs[:, CHUNK - 1:CHUNK], acs.shape)
    log_dt = jnp.log(dt)
    grow_f = jnp.exp(acs)
    xscale_f = dt * jnp.exp(tot - acs)
    grow_b = jnp.exp(tot - ecs)
    xscale_b = dt * jnp.exp(ecs)
    for c in range(n_chunks):
        fwd = slice(2 * nh * c, 2 * nh * c + nh)
        bwd = slice(2 * nh * c + nh, 2 * nh * (c + 1))
        rows_out[0, c] = jnp.concatenate(
            [acs[fwd] - log_dt[fwd],
             ecs[bwd] + log_dt[bwd],
             jnp.log(dt[fwd] + dt[bwd])], axis=0)
        tok = slice(c * CHUNK, (c + 1) * CHUNK)
        cols_out[0, tok, :] = _pad_rows(jnp.concatenate([acs[fwd], ecs[bwd]], axis=0), CHUNK).T
        wide = jnp.concatenate([grow_f[fwd], xscale_f[fwd], grow_b[bwd], xscale_b[bwd]], axis=0)
        wide_hi = wide.astype(BF16).astype(F32)
        packed_out[0, tok, :] = jnp.concatenate([wide_hi, wide - wide_hi], axis=0).T.astype(BF16)


SSD_STEP_CHUNKS = 8


def _ssd_kernel(xf_ref, xb_ref, zf_ref, zb_ref, rows_ref, cols_ref, pf_ref, pb_ref, dexp_ref, e_ref,
                yf_ref, yb_ref, *state_refs):
    hf_ref = state_refs[:SSD_GROUPS]
    hb_ref = state_refs[SSD_GROUPS:]
    d_ssd = SSD_HEADS * SSD_HEADDIM
    b_off = d_ssd
    c_off = d_ssd + SSD_GROUPS * SSD_STATE
    nh = SSD_HEADS

    @pl.when(pl.program_id(1) == 0)
    def _():
        for ref in state_refs:
            ref[...] = jnp.zeros_like(ref)

    row_i = lax.broadcasted_iota(jnp.int32, (CHUNK, CHUNK), 0)
    col_i = lax.broadcasted_iota(jnp.int32, (CHUNK, CHUNK), 1)
    below = row_i > col_i
    above = row_i < col_i
    head_lane = lax.broadcasted_iota(jnp.int32, (CHUNK, GROUP_WIDTH), 1) // SSD_HEADDIM

    for k in range(SSD_STEP_CHUNKS):
        tf = slice(k * CHUNK, (k + 1) * CHUNK)
        tb = slice((SSD_STEP_CHUNKS - 1 - k) * CHUNK, (SSD_STEP_CHUNKS - k) * CHUNK)
        row_f = rows_ref[0, k, 0:nh, :]
        row_b = rows_ref[0, k, nh:2 * nh, :]
        row_d = rows_ref[0, k, 2 * nh:3 * nh, :]
        cols = cols_ref[0, tf, :]
        packed_f = pf_ref[0, tf, :]
        packed_b = pb_ref[0, tb, :]

        for g in range(SSD_GROUPS):
            gs = slice(g * GROUP_WIDTH, (g + 1) * GROUP_WIDTH)
            bs = slice(b_off + g * SSD_STATE, b_off + (g + 1) * SSD_STATE)
            cs = slice(c_off + g * SSD_STATE, c_off + (g + 1) * SSD_STATE)
            grow_f = _dot(packed_f, e_ref[0, :, gs])
            xscale_f = _dot(packed_f, e_ref[1, :, gs])
            grow_b = _dot(packed_b, e_ref[2, :, gs])
            xscale_b = _dot(packed_b, e_ref[3, :, gs])
            hdecay_f = grow_f[CHUNK - 1:CHUNK, :]
            hdecay_b = grow_b[0:1, :]

            xs = xf_ref[0, tf, gs]
            bm = xf_ref[0, tf, bs]
            cm = xf_ref[0, tf, cs]
            scores = _dot_nt(cm, bm)
            w_heads = []
            x_blocks = []
            for r in range(HEADS_PER_GROUP):
                hd = g * HEADS_PER_GROUP + r
                expo = jnp.where(below, cols[:, hd:hd + 1] - row_f[hd:hd + 1, :],
                                 jnp.where(above, row_b[hd:hd + 1, :] - cols[:, nh + hd:nh + hd + 1],
                                           row_d[hd:hd + 1, :]))
                w_heads.append((scores * jnp.exp(expo)).astype(BF16))
                x_blocks.append(jnp.where(head_lane == r, xs, jnp.zeros_like(xs)))
            xs32 = xs.astype(F32)
            y = _dot(jnp.concatenate(w_heads, axis=1), jnp.concatenate(x_blocks, axis=0))
            y = y + xs32 * dexp_ref[:, gs] + _dot(cm, hf_ref[g][...].astype(BF16)) * grow_f
            yf_ref[0, tf, gs] = (y * _silu(zf_ref[0, tf, gs].astype(F32))).astype(yf_ref.dtype)
            xt = (xs32 * xscale_f).astype(BF16)
            hf_ref[g][...] = hf_ref[g][...] * hdecay_f + _dot_tn(bm, xt)

            xs = xb_ref[0, tb, gs]
            bm = xb_ref[0, tb, bs]
            cm = xb_ref[0, tb, cs]
            y = _dot(cm, hb_ref[g][...].astype(BF16)) * grow_b
            yb_ref[0, tb, gs] = (y * _silu(zb_ref[0, tb, gs].astype(F32))).astype(yb_ref.dtype)
            xt = (xs.astype(F32) * xscale_b).astype(BF16)
            hb_ref[g][...] = hb_ref[g][...] * hdecay_b + _dot_tn(bm, xt)


def _ssd(xbc, z, rows, cols, packed, layer, d_expand, expanders):
    bsz, length, width = xbc.shape
    span = SSD_STEP_CHUNKS * CHUNK
    n_steps = length // span
    d_ssd = SSD_HEADS * SSD_HEADDIM
    fwd = lambda b, i: (b, i, 0)
    bwd = lambda b, i: (b, n_steps - 1 - i, 0)
    in_specs = [
        pl.BlockSpec((1, span, width), fwd),
        pl.BlockSpec((1, span, width), bwd),
        pl.BlockSpec((1, span, d_ssd), fwd),
        pl.BlockSpec((1, span, d_ssd), bwd),
        pl.BlockSpec((1, SSD_STEP_CHUNKS, 3 * SSD_HEADS, CHUNK), lambda b, i: (b, i, 0, 0)),
        pl.BlockSpec((1, span, LANES), fwd),
        pl.BlockSpec((1, span, LANES), fwd),
        pl.BlockSpec((1, span, LANES), bwd),
        _layer_spec(d_expand, layer), _const_spec(expanders.shape),
    ]
    out_shape = (jax.ShapeDtypeStruct((bsz, length, d_ssd), BF16),
                 jax.ShapeDtypeStruct((bsz, length, d_ssd), BF16))
    out_specs = (pl.BlockSpec((1, span, d_ssd), fwd), pl.BlockSpec((1, span, d_ssd), bwd))
    state = pltpu.VMEM((SSD_STATE, GROUP_WIDTH), F32)
    return pl.pallas_call(
        _ssd_kernel, grid=(bsz, n_steps), in_specs=in_specs, out_specs=out_specs,
        out_shape=out_shape, scratch_shapes=[state] * (2 * SSD_GROUPS),
        compiler_params=pltpu.CompilerParams(
            dimension_semantics=("parallel", "arbitrary"), vmem_limit_bytes=VMEM_LIMIT_BYTES),
        name="ssd",
    )(xbc, xbc, z, z, rows, cols, packed, packed, d_expand, expanders)


MIXOUT_TILE = 1024
MIXOUT_ROW_BLOCK = 512


def _mixout_kernel(h_ref, yf_ref, yb_ref, ysc_ref, gw_ref, wo_ref, out_ref):
    d_ssd = yf_ref.shape[-1]
    rows = h_ref.shape[1]
    for r in range(0, rows, MIXOUT_ROW_BLOCK):
        rs = slice(r, min(r + MIXOUT_ROW_BLOCK, rows))
        y = yf_ref[0, rs, :].astype(F32) + yb_ref[0, rs, :].astype(F32)
        normed = []
        for g in range(SSD_GROUPS):
            gs = slice(g * GROUP_WIDTH, (g + 1) * GROUP_WIDTH)
            normed.append(_rms(y[:, gs], gw_ref[:, gs]).astype(BF16))
        yn = jnp.concatenate(normed, axis=1)
        out = _dot(yn, wo_ref[0:d_ssd, :]) + _dot(ysc_ref[0, rs, :], wo_ref[d_ssd:, :])
        out_ref[0, rs, :] = h_ref[0, rs, :] + out


def _mixout(h, y_f, y_b, y_sc, layer, gnorm_w, w_out, *, tile):
    bsz, length, d = h.shape
    row_spec = lambda width: pl.BlockSpec((1, tile, width), lambda b, t: (b, t, 0))
    in_specs = [row_spec(d), row_spec(y_f.shape[-1]), row_spec(y_b.shape[-1]), row_spec(y_sc.shape[-1]),
                _layer_spec(gnorm_w, layer), _layer_spec(w_out, layer)]
    return pl.pallas_call(
        _mixout_kernel, grid=(bsz, length // tile), in_specs=in_specs, out_specs=row_spec(d),
        out_shape=jax.ShapeDtypeStruct(h.shape, F32),
        compiler_params=pltpu.CompilerParams(
            dimension_semantics=("parallel", "parallel"), vmem_limit_bytes=VMEM_LIMIT_BYTES),
        name="mixout",
    )(h, y_f, y_b, y_sc, gnorm_w, w_out)


FF_COL_BLOCK = 256
FFN_TILE = 1024


def _strand_conv3(p_ref, w_ref, rows):
    sub = lax.broadcasted_iota(jnp.int32, (F32_SUBLANES, p_ref.shape[1]), 0)
    width = p_ref.shape[1]
    before = jnp.broadcast_to(p_ref[rows:rows + 1, :], (F32_SUBLANES, width))
    after = jnp.broadcast_to(p_ref[rows + 1:rows + 2, :], (F32_SUBLANES, width))
    head = jnp.where(sub == 0, before, pltpu.roll(p_ref[rows - F32_SUBLANES:rows, :], 1, 0))
    tail = jnp.where(sub == F32_SUBLANES - 1, after,
                     pltpu.roll(p_ref[0:F32_SUBLANES, :], F32_SUBLANES - 1, 0))
    prev = jnp.concatenate([head, p_ref[0:rows - F32_SUBLANES, :]], axis=0)
    nxt = jnp.concatenate([p_ref[F32_SUBLANES:rows, :], tail], axis=0)
    return prev * w_ref[0:1, :] + p_ref[0:rows, :] * w_ref[1:2, :] + nxt * w_ref[2:3, :]


def _strand_blocks(strand, d):
    return [(s, w, j) for s in range(F32_SUBLANES) for w in range(strand // F32_SUBLANES)
            for j in range(d // LANES)]


def _ffn_kernel(hp_ref, h_ref, hn_ref, nw_ref, wup_ref, cw_ref, cb_ref, wdn_ref, fw_ref,
                out_ref, hs_ref, lhs_ref, act_ref, p_ref, *, final_norm):
    rows = h_ref.shape[1]
    d = h_ref.shape[2]
    strand = rows // F32_SUBLANES
    d_ff = wdn_ref.shape[0]
    t = pl.program_id(1)
    last = pl.num_programs(1) - 1
    norm_w = nw_ref[...]

    for s, w, j in _strand_blocks(strand, d):
        tok = strand * s + F32_SUBLANES * w
        hs_ref[j, pl.ds(F32_SUBLANES * F32_SUBLANES * w + s, F32_SUBLANES, stride=F32_SUBLANES), :] = (
            h_ref[0, tok:tok + F32_SUBLANES, j * LANES:(j + 1) * LANES])
    h_strand = jnp.concatenate([hs_ref[j] for j in range(d // LANES)], axis=1)
    before = _rms(hp_ref[0], norm_w)[F32_SUBLANES - 1:F32_SUBLANES] * (t > 0).astype(F32)
    after = _rms(hn_ref[0], norm_w)[0:1] * (t < last).astype(F32)
    halo_row = lax.broadcasted_iota(jnp.int32, (HALO, d), 0)
    halo = jnp.where(halo_row == 0, jnp.broadcast_to(before, (HALO, d)),
                     jnp.where(halo_row == 1, jnp.broadcast_to(after, (HALO, d)), 0.0))
    lhs_ref[...] = jnp.concatenate([_rms(h_strand, norm_w), halo], axis=0).astype(BF16)
    full = lhs_ref[...]

    slot = 0
    for c in range(0, d_ff, FF_COL_BLOCK):
        gs = slice(c, c + FF_COL_BLOCK)
        us = slice(d_ff + c, d_ff + c + FF_COL_BLOCK)
        gbuf = p_ref.at[slot, 0]
        ubuf = p_ref.at[slot, 1]
        slot = 1 - slot
        gbuf[...] = _dot(full, wup_ref[:, gs])
        ubuf[...] = _dot(full, wup_ref[:, us])
        gate = _strand_conv3(gbuf, cw_ref.at[:, gs], rows) + cb_ref[:, gs]
        up = _strand_conv3(ubuf, cw_ref.at[:, us], rows) + cb_ref[:, us]
        act_ref[:, gs] = (_silu(gate) * up).astype(BF16)
    out = h_strand + _dot(act_ref[...], wdn_ref[...])
    if final_norm:
        out = _rms(out, fw_ref[...])
    for j in range(d // LANES):
        hs_ref[j] = out[:, j * LANES:(j + 1) * LANES]
    for s, w, j in _strand_blocks(strand, d):
        tok = strand * s + F32_SUBLANES * w
        out_ref[0, tok:tok + F32_SUBLANES, j * LANES:(j + 1) * LANES] = (
            hs_ref[j, pl.ds(F32_SUBLANES * F32_SUBLANES * w + s, F32_SUBLANES, stride=F32_SUBLANES), :])


def _ffn(h, layer, norm_w, w_up, conv_w, conv_b, w_down, final_w, *, tile, final_norm):
    bsz, length, d = h.shape
    d_ff = w_down.shape[1]
    blk8 = tile // F32_SUBLANES
    last8 = length // F32_SUBLANES - 1
    row_spec = pl.BlockSpec((1, tile, d), lambda b, t: (b, t, 0))
    in_specs = [
        pl.BlockSpec((1, F32_SUBLANES, d), lambda b, t: (b, jnp.maximum(t * blk8 - 1, 0), 0)),
        row_spec,
        pl.BlockSpec((1, F32_SUBLANES, d), lambda b, t: (b, jnp.minimum((t + 1) * blk8, last8), 0)),
        _layer_spec(norm_w, layer), _layer_spec(w_up, layer), _layer_spec(conv_w, layer),
        _layer_spec(conv_b, layer), _layer_spec(w_down, layer), _const_spec(final_w.shape),
    ]
    return pl.pallas_call(
        functools.partial(_ffn_kernel, final_norm=final_norm),
        grid=(bsz, length // tile), in_specs=in_specs, out_specs=row_spec,
        out_shape=jax.ShapeDtypeStruct(h.shape, F32),
        scratch_shapes=[pltpu.VMEM((d // LANES, tile, LANES), F32), pltpu.VMEM((tile + HALO, d), BF16),
                        pltpu.VMEM((tile, d_ff), BF16),
                        pltpu.VMEM((2, 2, tile + HALO, FF_COL_BLOCK), F32)],
        compiler_params=pltpu.CompilerParams(
            dimension_semantics=("parallel", "parallel"), vmem_limit_bytes=VMEM_LIMIT_BYTES),
        name="ffn",
    )(h, h, h, norm_w, w_up, conv_w, conv_b, w_down, final_w)


def _lane_rows(fwd, bwd):
    v = jnp.concatenate([fwd, bwd], axis=-1).astype(F32)
    return jnp.broadcast_to(v[..., None], v.shape + (LANES,))


def _head_expanders():
    j = jnp.arange(LANES)[None, :, None]
    k = jnp.arange(N_EXPAND)[:, None, None]
    head = jnp.arange(SSD_HEADS * SSD_HEADDIM)[None, None, :] // SSD_HEADDIM
    half = N_EXPAND * SSD_HEADS
    return ((j == k * SSD_HEADS + head) | (j == half + k * SSD_HEADS + head)).astype(BF16)


def kernel(x, w_in, conv_xbc_w, conv_xbc_b, dt_bias_f, dt_bias_b, a_log_f, a_log_b, d_skip,
           ssd_norm_w, sc_conv_w, w_out, norm1_w, norm2_w, w_ffn_up, ffn_conv_w, ffn_conv_b,
           w_ffn_down, final_norm_w):
    depth = w_in.shape[0]
    d_ssd = SSD_HEADS * SSD_HEADDIM
    o_dt = d_ssd + d_ssd + 2 * SSD_GROUPS * SSD_STATE
    o_sc = o_dt + 2 * SSD_HEADS
    length = x.shape[1]
    w_head = w_in.astype(BF16)
    w_tail = jnp.concatenate(
        [w_in[..., o_sc:], jnp.pad(w_in[..., o_dt:o_sc], ((0, 0), (0, 0), (0, LANES - 2 * SSD_HEADS)))],
        axis=-1).astype(BF16)
    w_out_all = w_out.astype(BF16)
    w_up_all = w_ffn_up.astype(BF16)
    w_down_all = w_ffn_down.astype(BF16)
    rows3 = lambda v: v.astype(F32)[:, None, :]
    norm1, norm2, gnorm = rows3(norm1_w), rows3(norm2_w), rows3(ssd_norm_w)
    conv_b, ffn_b = rows3(conv_xbc_b), rows3(ffn_conv_b)
    dt_bias = _lane_rows(dt_bias_f, dt_bias_b)
    a_log = _lane_rows(a_log_f, a_log_b)
    d_expand = rows3(jnp.repeat(d_skip, SSD_HEADDIM, axis=-1))
    final_w = final_norm_w.astype(F32).reshape(1, -1)
    expanders = _head_expanders()

    h = x
    for l in range(depth):
        z, xbc, y_sc, rows, cols, packed = _inproj(
            h, l, norm1, w_head, w_tail, conv_xbc_w.astype(F32), conv_b, sc_conv_w.astype(F32), dt_bias, a_log,
            tile=min(INPROJ_TILE, length))
        y_f, y_b = _ssd(xbc, z, rows, cols, packed, l, d_expand, expanders)
        h = _mixout(h, y_f, y_b, y_sc, l, gnorm, w_out_all, tile=min(MIXOUT_TILE, length))
        h = _ffn(h, l, norm2, w_up_all, ffn_conv_w.astype(F32), ffn_b, w_down_all, final_w,
                 tile=min(FFN_TILE, length), final_norm=(l == depth - 1))
    return h
```

```python
import functools

import jax
import jax.numpy as jnp
from jax import lax
from jax.experimental import pallas as pl
from jax.experimental.pallas import tpu as pltpu

F32 = jnp.float32
BF16 = jnp.bfloat16

RMS_EPS = 1e-5
SSD_HEADDIM = 64
SSD_GROUPS = 4
SSD_STATE = 128
CHUNK = 128
HEADS_PER_GROUP = 4
SSD_HEADS = SSD_GROUPS * HEADS_PER_GROUP
GROUP_WIDTH = HEADS_PER_GROUP * SSD_HEADDIM

LANES = 128
F32_SUBLANES = 8
BF16_SUBLANES = 16
HALO = BF16_SUBLANES
VMEM_LIMIT_BYTES = 56 * 1024 * 1024


def _dot(a, b):
    return jnp.dot(a, b, preferred_element_type=F32)


def _dot_nt(a, b):
    return lax.dot_general(a, b, (((1,), (1,)), ((), ())), preferred_element_type=F32)


def _dot_tn(a, b):
    return lax.dot_general(a, b, (((0,), (0,)), ((), ())), preferred_element_type=F32)


def _silu(x):
    return x * (1.0 / (1.0 + jnp.exp(-x)))


def _rms(v, w):
    ms = jnp.mean(v * v, axis=-1, keepdims=True)
    return v * lax.rsqrt(ms + RMS_EPS) * w


def _const_spec(shape):
    zeros = (0,) * len(shape)
    return pl.BlockSpec(shape, lambda *_: zeros, pipeline_mode=pl.Buffered(1))


def _layer_spec(stacked, layer, cols=None, col_block=0):
    _, rows, width = stacked.shape
    return pl.BlockSpec((None, rows, width if cols is None else cols),
                        lambda *_: (layer, 0, col_block), pipeline_mode=pl.Buffered(1))


def _halo_lhs(prev_ref, main_ref, next_ref, norm_w, lhs_ref):
    t = pl.program_id(1)
    last = pl.num_programs(1) - 1
    d = main_ref.shape[-1]
    prev = _rms(prev_ref[0], norm_w) * (t > 0).astype(F32)
    nxt = _rms(next_ref[0], norm_w) * (t < last).astype(F32)
    main = _rms(main_ref[0], norm_w)
    zeros = jnp.zeros((HALO - F32_SUBLANES, d), F32)
    lhs_ref[...] = jnp.concatenate([zeros, prev, main, nxt, zeros], axis=0).astype(BF16)


def _conv3(p_ref, w_ref, rows):
    return (p_ref[HALO - 1:HALO - 1 + rows, :] * w_ref[0:1, :]
            + p_ref[HALO:HALO + rows, :] * w_ref[1:2, :]
            + p_ref[HALO + 1:HALO + 1 + rows, :] * w_ref[2:3, :])


IN_COL_BLOCK = 512
INPROJ_TILE = 1024


def _inproj_kernel(hp_ref, h_ref, hn_ref, nw_ref, wz_ref, wxa_ref, wxb_ref, wscb_ref, wscc_ref,
                   wscv_ref, wdt_ref, cw_ref, cb_ref, scw_ref, bias_ref, alog_ref,
                   z_out, xbc_out, ysc_out, rows_out, cols_out, packed_out, lhs_ref, p_ref):
    rows = h_ref.shape[1]
    _halo_lhs(hp_ref, h_ref, hn_ref, nw_ref[...], lhs_ref)
    centre = lhs_ref[HALO:HALO + rows, :]
    full = lhs_ref[...]

    _ssd_scalars(_dot(centre, wdt_ref[...]), bias_ref, alog_ref, rows_out, cols_out, packed_out)
    for c in range(0, wz_ref.shape[1], IN_COL_BLOCK):
        cs = slice(c, c + IN_COL_BLOCK)
        z_out[0, :, cs] = _silu(_dot(centre, wz_ref[:, cs])).astype(z_out.dtype)
    slot = 0
    half = wxa_ref.shape[1]
    for c in range(0, 2 * half, IN_COL_BLOCK):
        cs = slice(c, c + IN_COL_BLOCK)
        w_ref = wxa_ref if c < half else wxb_ref
        pbuf = p_ref.at[slot]
        slot = 1 - slot
        pbuf[...] = _dot(full, w_ref[:, c % half:c % half + IN_COL_BLOCK])
        conv = _conv3(pbuf, cw_ref.at[:, cs], rows) + cb_ref[:, cs]
        xbc_out[0, :, cs] = _silu(conv).astype(xbc_out.dtype)
    for c in range(0, wscb_ref.shape[1], IN_COL_BLOCK):
        cs = slice(c, c + IN_COL_BLOCK)
        pbuf = p_ref.at[slot]
        slot = 1 - slot
        pbuf[...] = _dot(full, wscc_ref[:, cs]) * _dot(full, wscv_ref[:, cs])
        gate = _dot(centre, wscb_ref[:, cs])
        ysc_out[0, :, cs] = (gate * _conv3(pbuf, scw_ref.at[:, cs], rows)).astype(ysc_out.dtype)


def _inproj(h, layer, norm_w, w_head, w_tail, conv_w, conv_b, sc_w, dt_bias, a_log, *, tile):
    bsz, length, d = h.shape
    n_tiles = length // tile
    blk8 = tile // F32_SUBLANES
    last8 = length // F32_SUBLANES - 1
    grid = (bsz, n_tiles)
    row_spec = lambda width: pl.BlockSpec((1, tile, width), lambda b, t: (b, t, 0))
    head = lambda j: _layer_spec(w_head, layer, cols=d, col_block=j)
    tail = lambda j: _layer_spec(w_tail, layer, cols=d, col_block=j)
    in_specs = [
        pl.BlockSpec((1, F32_SUBLANES, d), lambda b, t: (b, jnp.maximum(t * blk8 - 1, 0), 0)),
        row_spec(d),
        pl.BlockSpec((1, F32_SUBLANES, d), lambda b, t: (b, jnp.minimum((t + 1) * blk8, last8), 0)),
        _layer_spec(norm_w, layer),
        head(0), head(1), head(2), tail(0), tail(1), tail(2),
        _layer_spec(w_tail, layer, cols=LANES, col_block=3 * d // LANES),
        _layer_spec(conv_w, layer), _layer_spec(conv_b, layer), _layer_spec(sc_w, layer),
        _layer_spec(dt_bias, layer), _layer_spec(a_log, layer),
    ]
    out_shape = (
        jax.ShapeDtypeStruct((bsz, length, d), BF16),
        jax.ShapeDtypeStruct((bsz, length, 2 * d), BF16),
        jax.ShapeDtypeStruct((bsz, length, d), BF16),
        jax.ShapeDtypeStruct((bsz, length // CHUNK, 3 * SSD_HEADS, CHUNK), F32),
        jax.ShapeDtypeStruct((bsz, length, LANES), F32),
        jax.ShapeDtypeStruct((bsz, length, LANES), BF16),
    )
    out_specs = (row_spec(d), row_spec(2 * d), row_spec(d),
                 pl.BlockSpec((1, tile // CHUNK, 3 * SSD_HEADS, CHUNK), lambda b, t: (b, t, 0, 0)),
                 row_spec(LANES), row_spec(LANES))
    return pl.pallas_call(
        _inproj_kernel, grid=grid, in_specs=in_specs, out_specs=out_specs, out_shape=out_shape,
        scratch_shapes=[pltpu.VMEM((tile + 2 * HALO, d), BF16),
                        pltpu.VMEM((2, tile + 2 * HALO, IN_COL_BLOCK), F32)],
        compiler_params=pltpu.CompilerParams(
            dimension_semantics=("parallel", "parallel"), vmem_limit_bytes=VMEM_LIMIT_BYTES),
        name="inproj",
    )(h, h, h, norm_w, w_head, w_head, w_head, w_tail, w_tail, w_tail, w_tail, conv_w, conv_b, sc_w,
      dt_bias, a_log)


N_EXPAND = 4


def _softplus(x):
    return jnp.maximum(x, 0.0) + jnp.log(1.0 + jnp.exp(-jnp.abs(x)))


def _cumsum_lanes(a):
    lane = lax.broadcasted_iota(jnp.int32, a.shape, 1)
    shift = 1
    while shift < a.shape[1]:
        a = a + jnp.where(lane >= shift, pltpu.roll(a, shift, 1), 0.0)
        shift *= 2
    return a


def _pad_rows(v, rows):
    return jnp.concatenate([v, jnp.zeros((rows - v.shape[0], v.shape[1]), v.dtype)], axis=0)


def _ssd_scalars(dt_raw, bias_ref, alog_ref, rows_out, cols_out, packed_out):
    nh = SSD_HEADS
    n_chunks = dt_raw.shape[0] // CHUNK
    raw = jnp.concatenate(
        [dt_raw[c * CHUNK:(c + 1) * CHUNK, :].T[0:2 * nh] for c in range(n_chunks)], axis=0)
    bias = jnp.concatenate([bias_ref[...]] * n_chunks, axis=0)
    neg_a = -jnp.exp(jnp.concatenate([alog_ref[...]] * n_chunks, axis=0))
    dt = _softplus(raw + bias)
    a = dt * neg_a
    acs = _cumsum_lanes(a)
    ecs = acs - a
    tot = jnp.broadcast_to(acs[:, CHUNK - 1:CHUNK], acs.shape)
    log_dt = jnp.log(dt)
    grow_f = jnp.exp(acs)
    xscale_f = dt * jnp.exp(tot - acs)
    grow_b = jnp.exp(tot - ecs)
    xscale_b = dt * jnp.exp(ecs)
    for c in range(n_chunks):
        fwd = slice(2 * nh * c, 2 * nh * c + nh)
        bwd = slice(2 * nh * c + nh, 2 * nh * (c + 1))
        rows_out[0, c] = jnp.concatenate(
            [acs[fwd] - log_dt[fwd],
             ecs[bwd] + log_dt[bwd],
             jnp.log(dt[fwd] + dt[bwd])], axis=0)
        tok = slice(c * CHUNK, (c + 1) * CHUNK)
        cols_out[0, tok, :] = _pad_rows(jnp.concatenate([acs[fwd], ecs[bwd]], axis=0), CHUNK).T
        wide = jnp.concatenate([grow_f[fwd], xscale_f[fwd], grow_b[bwd], xscale_b[bwd]], axis=0)
        wide_hi = wide.astype(BF16).astype(F32)
        packed_out[0, tok, :] = jnp.concatenate([wide_hi, wide - wide_hi], axis=0).T.astype(BF16)


SSD_STEP_CHUNKS = 8


def _ssd_kernel(xf_ref, xb_ref, zf_ref, zb_ref, rows_ref, cols_ref, pf_ref, pb_ref, dexp_ref, e_ref,
                yf_ref, yb_ref, *state_refs):
    hf_ref = state_refs[:SSD_GROUPS]
    hb_ref = state_refs[SSD_GROUPS:]
    d_ssd = SSD_HEADS * SSD_HEADDIM
    b_off = d_ssd
    c_off = d_ssd + SSD_GROUPS * SSD_STATE
    nh = SSD_HEADS

    @pl.when(pl.program_id(1) == 0)
    def _():
        for ref in state_refs:
            ref[...] = jnp.zeros_like(ref)

    row_i = lax.broadcasted_iota(jnp.int32, (CHUNK, CHUNK), 0)
    col_i = lax.broadcasted_iota(jnp.int32, (CHUNK, CHUNK), 1)
    below = row_i > col_i
    above = row_i < col_i
    head_lane = lax.broadcasted_iota(jnp.int32, (CHUNK, GROUP_WIDTH), 1) // SSD_HEADDIM

    for k in range(SSD_STEP_CHUNKS):
        tf = slice(k * CHUNK, (k + 1) * CHUNK)
        tb = slice((SSD_STEP_CHUNKS - 1 - k) * CHUNK, (SSD_STEP_CHUNKS - k) * CHUNK)
        row_f = rows_ref[0, k, 0:nh, :]
        row_b = rows_ref[0, k, nh:2 * nh, :]
        row_d = rows_ref[0, k, 2 * nh:3 * nh, :]
        cols = cols_ref[0, tf, :]
        packed_f = pf_ref[0, tf, :]
        packed_b = pb_ref[0, tb, :]

        for g in range(SSD_GROUPS):
            gs = slice(g * GROUP_WIDTH, (g + 1) * GROUP_WIDTH)
            bs = slice(b_off + g * SSD_STATE, b_off + (g + 1) * SSD_STATE)
            cs = slice(c_off + g * SSD_STATE, c_off + (g + 1) * SSD_STATE)
            grow_f = _dot(packed_f, e_ref[0, :, gs])
            xscale_f = _dot(packed_f, e_ref[1, :, gs])
            grow_b = _dot(packed_b, e_ref[2, :, gs])
            xscale_b = _dot(packed_b, e_ref[3, :, gs])
            hdecay_f = grow_f[CHUNK - 1:CHUNK, :]
            hdecay_b = grow_b[0:1, :]

            xs = xf_ref[0, tf, gs]
            bm = xf_ref[0, tf, bs]
            cm = xf_ref[0, tf, cs]
            scores = _dot_nt(cm, bm)
            w_heads = []
            x_blocks = []
            for r in range(HEADS_PER_GROUP):
                hd = g * HEADS_PER_GROUP + r
                expo = jnp.where(below, cols[:, hd:hd + 1] - row_f[hd:hd + 1, :],
                                 jnp.where(above, row_b[hd:hd + 1, :] - cols[:, nh + hd:nh + hd + 1],
                                           row_d[hd:hd + 1, :]))
                w_heads.append((scores * jnp.exp(expo)).astype(BF16))
                x_blocks.append(jnp.where(head_lane == r, xs, jnp.zeros_like(xs)))
            xs32 = xs.astype(F32)
            y = _dot(jnp.concatenate(w_heads, axis=1), jnp.concatenate(x_blocks, axis=0))
            y = y + xs32 * dexp_ref[:, gs] + _dot(cm, hf_ref[g][...].astype(BF16)) * grow_f
            yf_ref[0, tf, gs] = (y * zf_ref[0, tf, gs].astype(F32)).astype(yf_ref.dtype)
            xt = (xs32 * xscale_f).astype(BF16)
            hf_ref[g][...] = hf_ref[g][...] * hdecay_f + _dot_tn(bm, xt)

            xs = xb_ref[0, tb, gs]
            bm = xb_ref[0, tb, bs]
            cm = xb_ref[0, tb, cs]
            y = _dot(cm, hb_ref[g][...].astype(BF16)) * grow_b
            yb_ref[0, tb, gs] = (y * zb_ref[0, tb, gs].astype(F32)).astype(yb_ref.dtype)
            xt = (xs.astype(F32) * xscale_b).astype(BF16)
            hb_ref[g][...] = hb_ref[g][...] * hdecay_b + _dot_tn(bm, xt)


def _ssd(xbc, z, rows, cols, packed, layer, d_expand, expanders):
    bsz, length, width = xbc.shape
    span = SSD_STEP_CHUNKS * CHUNK
    n_steps = length // span
    d_ssd = SSD_HEADS * SSD_HEADDIM
    fwd = lambda b, i: (b, i, 0)
    bwd = lambda b, i: (b, n_steps - 1 - i, 0)
    in_specs = [
        pl.BlockSpec((1, span, width), fwd),
        pl.BlockSpec((1, span, width), bwd),
        pl.BlockSpec((1, span, d_ssd), fwd),
        pl.BlockSpec((1, span, d_ssd), bwd),
        pl.BlockSpec((1, SSD_STEP_CHUNKS, 3 * SSD_HEADS, CHUNK), lambda b, i: (b, i, 0, 0)),
        pl.BlockSpec((1, span, LANES), fwd),
        pl.BlockSpec((1, span, LANES), fwd),
        pl.BlockSpec((1, span, LANES), bwd),
        _layer_spec(d_expand, layer), _const_spec(expanders.shape),
    ]
    out_shape = (jax.ShapeDtypeStruct((bsz, length, d_ssd), BF16),
                 jax.ShapeDtypeStruct((bsz, length, d_ssd), BF16))
    out_specs = (pl.BlockSpec((1, span, d_ssd), fwd), pl.BlockSpec((1, span, d_ssd), bwd))
    state = pltpu.VMEM((SSD_STATE, GROUP_WIDTH), F32)
    return pl.pallas_call(
        _ssd_kernel, grid=(bsz, n_steps), in_specs=in_specs, out_specs=out_specs,
        out_shape=out_shape, scratch_shapes=[state] * (2 * SSD_GROUPS),
        compiler_params=pltpu.CompilerParams(
            dimension_semantics=("parallel", "arbitrary"), vmem_limit_bytes=VMEM_LIMIT_BYTES),
        name="ssd",
    )(xbc, xbc, z, z, rows, cols, packed, packed, d_expand, expanders)


MIXOUT_TILE = 1024
MIXOUT_ROW_BLOCK = 512


def _mixout_kernel(h_ref, yf_ref, yb_ref, ysc_ref, gw_ref, wo_ref, out_ref):
    d_ssd = yf_ref.shape[-1]
    rows = h_ref.shape[1]
    for r in range(0, rows, MIXOUT_ROW_BLOCK):
        rs = slice(r, min(r + MIXOUT_ROW_BLOCK, rows))
        y = yf_ref[0, rs, :].astype(F32) + yb_ref[0, rs, :].astype(F32)
        normed = []
        for g in range(SSD_GROUPS):
            gs = slice(g * GROUP_WIDTH, (g + 1) * GROUP_WIDTH)
            normed.append(_rms(y[:, gs], gw_ref[:, gs]).astype(BF16))
        yn = jnp.concatenate(normed, axis=1)
        out = _dot(yn, wo_ref[0:d_ssd, :]) + _dot(ysc_ref[0, rs, :], wo_ref[d_ssd:, :])
        out_ref[0, rs, :] = h_ref[0, rs, :] + out


def _mixout(h, y_f, y_b, y_sc, layer, gnorm_w, w_out, *, tile):
    bsz, length, d = h.shape
    row_spec = lambda width: pl.BlockSpec((1, tile, width), lambda b, t: (b, t, 0))
    in_specs = [row_spec(d), row_spec(y_f.shape[-1]), row_spec(y_b.shape[-1]), row_spec(y_sc.shape[-1]),
                _layer_spec(gnorm_w, layer), _layer_spec(w_out, layer)]
    return pl.pallas_call(
        _mixout_kernel, grid=(bsz, length // tile), in_specs=in_specs, out_specs=row_spec(d),
        out_shape=jax.ShapeDtypeStruct(h.shape, F32),
        compiler_params=pltpu.CompilerParams(
            dimension_semantics=("parallel", "parallel"), vmem_limit_bytes=VMEM_LIMIT_BYTES),
        name="mixout",
    )(h, y_f, y_b, y_sc, gnorm_w, w_out)


FF_COL_BLOCK = 256
FFN_TILE = 1024


def _strand_conv3(p_ref, w_ref, rows):
    sub = lax.broadcasted_iota(jnp.int32, (F32_SUBLANES, p_ref.shape[1]), 0)
    width = p_ref.shape[1]
    before = jnp.broadcast_to(p_ref[rows:rows + 1, :], (F32_SUBLANES, width))
    after = jnp.broadcast_to(p_ref[rows + 1:rows + 2, :], (F32_SUBLANES, width))
    head = jnp.where(sub == 0, before, pltpu.roll(p_ref[rows - F32_SUBLANES:rows, :], 1, 0))
    tail = jnp.where(sub == F32_SUBLANES - 1, after,
                     pltpu.roll(p_ref[0:F32_SUBLANES, :], F32_SUBLANES - 1, 0))
    prev = jnp.concatenate([head, p_ref[0:rows - F32_SUBLANES, :]], axis=0)
    nxt = jnp.concatenate([p_ref[F32_SUBLANES:rows, :], tail], axis=0)
    return prev * w_ref[0:1, :] + p_ref[0:rows, :] * w_ref[1:2, :] + nxt * w_ref[2:3, :]


def _strand_blocks(strand, d):
    return [(s, w, j) for s in range(F32_SUBLANES) for w in range(strand // F32_SUBLANES)
            for j in range(d // LANES)]


def _ffn_kernel(hp_ref, h_ref, hn_ref, nw_ref, wup_ref, cw_ref, cb_ref, wdn_ref, fw_ref,
                out_ref, hs_ref, lhs_ref, act_ref, p_ref, *, final_norm):
    rows = h_ref.shape[1]
    d = h_ref.shape[2]
    strand = rows // F32_SUBLANES
    d_ff = wdn_ref.shape[0]
    t = pl.program_id(1)
    last = pl.num_programs(1) - 1
    norm_w = nw_ref[...]

    for s, w, j in _strand_blocks(strand, d):
        tok = strand * s + F32_SUBLANES * w
        hs_ref[j, pl.ds(F32_SUBLANES * F32_SUBLANES * w + s, F32_SUBLANES, stride=F32_SUBLANES), :] = (
            h_ref[0, tok:tok + F32_SUBLANES, j * LANES:(j + 1) * LANES])
    h_strand = jnp.concatenate([hs_ref[j] for j in range(d // LANES)], axis=1)
    before = _rms(hp_ref[0], norm_w)[F32_SUBLANES - 1:F32_SUBLANES] * (t > 0).astype(F32)
    after = _rms(hn_ref[0], norm_w)[0:1] * (t < last).astype(F32)
    halo_row = lax.broadcasted_iota(jnp.int32, (HALO, d), 0)
    halo = jnp.where(halo_row == 0, jnp.broadcast_to(before, (HALO, d)),
                     jnp.where(halo_row == 1, jnp.broadcast_to(after, (HALO, d)), 0.0))
    lhs_ref[...] = jnp.concatenate([_rms(h_strand, norm_w), halo], axis=0).astype(BF16)
    full = lhs_ref[...]

    slot = 0
    for c in range(0, d_ff, FF_COL_BLOCK):
        gs = slice(c, c + FF_COL_BLOCK)
        us = slice(d_ff + c, d_ff + c + FF_COL_BLOCK)
        gbuf = p_ref.at[slot, 0]
        ubuf = p_ref.at[slot, 1]
        slot = 1 - slot
        gbuf[...] = _dot(full, wup_ref[:, gs])
        ubuf[...] = _dot(full, wup_ref[:, us])
        gate = _strand_conv3(gbuf, cw_ref.at[:, gs], rows) + cb_ref[:, gs]
        up = _strand_conv3(ubuf, cw_ref.at[:, us], rows) + cb_ref[:, us]
        act_ref[:, gs] = (_silu(gate) * up).astype(BF16)
    out = h_strand + _dot(act_ref[...], wdn_ref[...])
    if final_norm:
        out = _rms(out, fw_ref[...])
    for j in range(d // LANES):
        hs_ref[j] = out[:, j * LANES:(j + 1) * LANES]
    for s, w, j in _strand_blocks(strand, d):
        tok = strand * s + F32_SUBLANES * w
        out_ref[0, tok:tok + F32_SUBLANES, j * LANES:(j + 1) * LANES] = (
            hs_ref[j, pl.ds(F32_SUBLANES * F32_SUBLANES * w + s, F32_SUBLANES, stride=F32_SUBLANES), :])


def _ffn(h, layer, norm_w, w_up, conv_w, conv_b, w_down, final_w, *, tile, final_norm):
    bsz, length, d = h.shape
    d_ff = w_down.shape[1]
    blk8 = tile // F32_SUBLANES
    last8 = length // F32_SUBLANES - 1
    row_spec = pl.BlockSpec((1, tile, d), lambda b, t: (b, t, 0))
    in_specs = [
        pl.BlockSpec((1, F32_SUBLANES, d), lambda b, t: (b, jnp.maximum(t * blk8 - 1, 0), 0)),
        row_spec,
        pl.BlockSpec((1, F32_SUBLANES, d), lambda b, t: (b, jnp.minimum((t + 1) * blk8, last8), 0)),
        _layer_spec(norm_w, layer), _layer_spec(w_up, layer), _layer_spec(conv_w, layer),
        _layer_spec(conv_b, layer), _layer_spec(w_down, layer), _const_spec(final_w.shape),
    ]
    return pl.pallas_call(
        functools.partial(_ffn_kernel, final_norm=final_norm),
        grid=(bsz, length // tile), in_specs=in_specs, out_specs=row_spec,
        out_shape=jax.ShapeDtypeStruct(h.shape, F32),
        scratch_shapes=[pltpu.VMEM((d // LANES, tile, LANES), F32), pltpu.VMEM((tile + HALO, d), BF16),
                        pltpu.VMEM((tile, d_ff), BF16),
                        pltpu.VMEM((2, 2, tile + HALO, FF_COL_BLOCK), F32)],
        compiler_params=pltpu.CompilerParams(
            dimension_semantics=("parallel", "parallel"), vmem_limit_bytes=VMEM_LIMIT_BYTES),
        name="ffn",
    )(h, h, h, norm_w, w_up, conv_w, conv_b, w_down, final_w)


def _lane_rows(fwd, bwd):
    v = jnp.concatenate([fwd, bwd], axis=-1).astype(F32)
    return jnp.broadcast_to(v[..., None], v.shape + (LANES,))


def _head_expanders():
    j = jnp.arange(LANES)[None, :, None]
    k = jnp.arange(N_EXPAND)[:, None, None]
    head = jnp.arange(SSD_HEADS * SSD_HEADDIM)[None, None, :] // SSD_HEADDIM
    half = N_EXPAND * SSD_HEADS
    return ((j == k * SSD_HEADS + head) | (j == half + k * SSD_HEADS + head)).astype(BF16)


def kernel(x, w_in, conv_xbc_w, conv_xbc_b, dt_bias_f, dt_bias_b, a_log_f, a_log_b, d_skip,
           ssd_norm_w, sc_conv_w, w_out, norm1_w, norm2_w, w_ffn_up, ffn_conv_w, ffn_conv_b,
           w_ffn_down, final_norm_w):
    depth = w_in.shape[0]
    d_ssd = SSD_HEADS * SSD_HEADDIM
    o_dt = d_ssd + d_ssd + 2 * SSD_GROUPS * SSD_STATE
    o_sc = o_dt + 2 * SSD_HEADS
    length = x.shape[1]
    w_head = w_in.astype(BF16)
    w_tail = jnp.concatenate(
        [w_in[..., o_sc:], jnp.pad(w_in[..., o_dt:o_sc], ((0, 0), (0, 0), (0, LANES - 2 * SSD_HEADS)))],
        axis=-1).astype(BF16)
    w_out_all = w_out.astype(BF16)
    w_up_all = w_ffn_up.astype(BF16)
    w_down_all = w_ffn_down.astype(BF16)
    rows3 = lambda v: v.astype(F32)[:, None, :]
    norm1, norm2, gnorm = rows3(norm1_w), rows3(norm2_w), rows3(ssd_norm_w)
    conv_b, ffn_b = rows3(conv_xbc_b), rows3(ffn_conv_b)
    dt_bias = _lane_rows(dt_bias_f, dt_bias_b)
    a_log = _lane_rows(a_log_f, a_log_b)
    d_expand = rows3(jnp.repeat(d_skip, SSD_HEADDIM, axis=-1))
    final_w = final_norm_w.astype(F32).reshape(1, -1)
    expanders = _head_expanders()

    h = x
    for l in range(depth):
        z, xbc, y_sc, rows, cols, packed = _inproj(
            h, l, norm1, w_head, w_tail, conv_xbc_w.astype(F32), conv_b, sc_conv_w.astype(F32), dt_bias, a_log,
            tile=min(INPROJ_TILE, length))
        y_f, y_b = _ssd(xbc, z, rows, cols, packed, l, d_expand, expanders)
        h = _mixout(h, y_f, y_b, y_sc, l, gnorm, w_out_all, tile=min(MIXOUT_TILE, length))
        h = _ffn(h, l, norm2, w_up_all, ffn_conv_w.astype(F32), ffn_b, w_down_all, final_w,
                 tile=min(FFN_TILE, length), final_norm=(l == depth - 1))
    return h
```
